```python
import math
import jax
import jax.numpy as jnp
from jax import lax
import numpy as np

D_MODEL = 1024
BATCH = 2
SEQ = 8192
DEPTH = 4
DEC_BATCH = 128
DEC_SEQ = 8
PAST_LEN = 8192
PAGE_SIZE = 128

N_EVEN = (DEPTH + 1) // 2
N_ODD = DEPTH // 2
ALPHA = (2.0 * DEPTH) ** 0.25
BETA = (8.0 * DEPTH) ** -0.25
LN_EPS = 1e-5
RMS_EPS = 1e-6
NEG_INF = -1e30
Q_BLOCK = 128
FFN_HIDDEN = ((8 * D_MODEL // 3 + 127) // 128) * 128
NSA_HEADS = 8
NSA_DK = 64
CMP_LEN = 32
CMP_STRIDE = 16
CMP_HIDDEN = 2 * NSA_DK
SLC_LEN = 64
N_SELECT = 16
WINDOW = 512
FORCED_SCORE = 1e9
MLA_HEADS = 8
MLA_Q_RANK = 256
MLA_KV_RANK = 128
MLA_NOPE = 64
MLA_ROPE = 32
MLA_V = 64
ROPE_THETA = 10000.0
DIFF_HEADS = 8
DIFF_KV_HEADS = 2
DIFF_D = 64
DIFF_REP = DIFF_HEADS // DIFF_KV_HEADS
EVEN_SPLITS = (NSA_HEADS * NSA_DK, 4 * NSA_DK, 2 * NSA_DK, 3 * NSA_HEADS, MLA_Q_RANK, MLA_KV_RANK, MLA_ROPE)
EVEN_IN = sum(EVEN_SPLITS)
EVEN_MIX = NSA_HEADS * NSA_DK + MLA_HEADS * MLA_V
ODD_MIX = DIFF_HEADS * 2 * DIFF_D
ODD_IN = ODD_MIX + 2 * DIFF_KV_HEADS * 2 * DIFF_D

kernel_name = 'hybrid_nsa_mla_diffattn_decoder'


def layer_norm(x, g, b):
    xf = x.astype(jnp.float32)
    mu = jnp.mean(xf, -1, keepdims=True)
    var = jnp.mean(jnp.square(xf - mu), -1, keepdims=True)
    return ((xf - mu) * lax.rsqrt(var + LN_EPS)).astype(x.dtype) * g + b


def rms_norm(x, g):
    xf = x.astype(jnp.float32)
    return (xf * lax.rsqrt(jnp.mean(jnp.square(xf), -1, keepdims=True) + RMS_EPS)).astype(x.dtype) * g


def masked_softmax(s, mask):
    p = jax.nn.softmax(jnp.where(mask, s, NEG_INF), axis=-1)
    return jnp.where(mask, p, 0.0)


def alibi_slopes(n_heads):
    return jnp.asarray([2.0 ** (-8.0 * (h + 1) / n_heads) for h in range(n_heads)], jnp.float32)


def apply_rope(x, pos):
    half = x.shape[-1] // 2
    inv = ROPE_THETA ** (-jnp.arange(half, dtype=jnp.float32) / half)
    ang = pos.astype(jnp.float32)[:, None] * inv[None, :]
    cos = jnp.cos(ang)[None, :, None, :].astype(x.dtype)
    sin = jnp.sin(ang)[None, :, None, :].astype(x.dtype)
    x1, x2 = x[..., :half], x[..., half:]
    return jnp.concatenate([x1 * cos - x2 * sin, x1 * sin + x2 * cos], axis=-1)


def half_swiglu(x, w_in, w_out):
    g, u = jnp.split(x @ w_in, 2, axis=-1)
    return 0.5 * ((jax.nn.silu(g) * u) @ w_out)


def over_query_blocks(fn, n_q):
    starts = jnp.arange(n_q // Q_BLOCK, dtype=jnp.int32) * Q_BLOCK
    out = jnp.moveaxis(lax.map(fn, starts), 0, 1)
    return out.reshape(out.shape[0], n_q, out.shape[-1])


def nsa_compress(rows, pe, w1, w2):
    b, l, dk = rows.shape
    n_chunks = l // CMP_STRIDE
    chunks = rows[:, :n_chunks * CMP_STRIDE].reshape(b, n_chunks, CMP_STRIDE, dk)
    w1r = w1.reshape(2, CMP_STRIDE, dk, CMP_HIDDEN)
    first = jnp.einsum('bnrd,rdh->bnh', chunks, w1r[0])
    second = jnp.einsum('bnrd,rdh->bnh', chunks, w1r[1])
    hid = first[:, :-1] + second[:, 1:] + jnp.einsum('rd,rdh->h', pe, w1)
    tok = jax.nn.gelu(hid) @ w2
    pos = jnp.arange(n_chunks - 1, dtype=jnp.int32) * CMP_STRIDE + (CMP_LEN - 1)
    return tok, pos


def cmp_to_slc(n_cmp, n_slc):
    cs = jnp.arange(n_cmp)[:, None] * CMP_STRIDE
    bs = jnp.arange(n_slc)[None, :] * SLC_LEN
    return ((cs < bs + SLC_LEN) & (cs + CMP_LEN > bs)).astype(jnp.float32)


def nsa_keyside(rows, cmp_pe, cmp_w1, cmp_w2):
    b, l = rows.shape[:2]
    c_k, c_pos = nsa_compress(rows[:, :, 0], cmp_pe[0], cmp_w1[0], cmp_w2[0])
    c_v, _ = nsa_compress(rows[:, :, 1], cmp_pe[1], cmp_w1[1], cmp_w2[1])
    pad = (-l) % SLC_LEN
    sel = jnp.pad(rows[:, :, 2:], ((0, 0), (0, pad), (0, 0), (0, 0)))
    n_slc = (l + pad) // SLC_LEN
    sel = sel.reshape(b, n_slc, SLC_LEN, 2, NSA_DK)
    return c_k, c_v, c_pos, sel[:, :, :, 0], sel[:, :, :, 1], cmp_to_slc(c_k.shape[1], n_slc)


def nsa_core(q, gates, q_pos, c_k, c_v, c_pos, s_k, s_v, w_k, w_v, w_pos, cmap, slopes):
    scale = NSA_DK ** -0.5
    slope = slopes[None, :, None, None]
    dist_c = q_pos[:, None] - c_pos[None, :]
    sc = jnp.einsum('bqhd,bnd->bhqn', q, c_k).astype(jnp.float32) * scale - slope * dist_c.astype(jnp.float32)
    p_c = masked_softmax(sc, (dist_c >= 0)[None, None])
    o_c = jnp.einsum('bhqn,bnd->bqhd', p_c.astype(q.dtype), c_v)
    n_slc = s_k.shape[1]
    imp = jnp.einsum('bhqn,nj->bqj', p_c, cmap)
    blk = jnp.arange(n_slc)[None, :]
    cur = (q_pos // SLC_LEN)[:, None]
    forced = (blk == 0) | (blk == cur) | (blk == cur - 1)
    valid = blk * SLC_LEN <= q_pos[:, None]
    imp = jnp.where(valid, jnp.where(forced, FORCED_SCORE, imp), NEG_INF)
    k_sel = min(N_SELECT, n_slc)
    _, idx = lax.top_k(imp, k_sel)
    bsz, tq = idx.shape[:2]
    gk = jax.vmap(lambda blocks, ib: blocks[ib])(s_k, idx).reshape(bsz, tq, k_sel * SLC_LEN, NSA_DK)
    gv = jax.vmap(lambda blocks, ib: blocks[ib])(s_v, idx).reshape(bsz, tq, k_sel * SLC_LEN, NSA_DK)
    s_pos = (idx[..., None] * SLC_LEN + jnp.arange(SLC_LEN)).reshape(bsz, tq, k_sel * SLC_LEN)
    dist_s = q_pos[None, :, None] - s_pos
    ss = jnp.einsum('bqhd,bqkd->bhqk', q, gk).astype(jnp.float32) * scale - slope * dist_s[:, None].astype(jnp.float32)
    p_s = masked_softmax(ss, (dist_s >= 0)[:, None])
    o_s = jnp.einsum('bhqk,bqkd->bqhd', p_s.astype(q.dtype), gv)
    dist_w = q_pos[:, None] - w_pos[None, :]
    sw = jnp.einsum('bqhd,bkd->bhqk', q, w_k).astype(jnp.float32) * scale - slope * dist_w.astype(jnp.float32)
    mask_w = (dist_w >= 0) & (dist_w <= WINDOW) & (w_pos >= 0)[None, :]
    p_w = masked_softmax(sw, mask_w[None, None])
    o_w = jnp.einsum('bhqk,bkd->bqhd', p_w.astype(q.dtype), w_v)
    g = jax.nn.sigmoid(gates.astype(jnp.float32)).astype(q.dtype)
    return g[:, :, 0, :, None] * o_c + g[:, :, 1, :, None] * o_s + g[:, :, 2, :, None] * o_w


def mla_core(q_nope, q_rope, q_pos, kv_rows, k_pos, w_uk, w_uv):
    c = kv_rows[..., :MLA_KV_RANK]
    kr = kv_rows[..., MLA_KV_RANK:]
    q_abs = jnp.einsum('bqhn,hrn->bqhr', q_nope, w_uk)
    s = (jnp.einsum('bqhr,bkr->bhqk', q_abs, c) + jnp.einsum('bqhe,bke->bhqk', q_rope, kr)).astype(jnp.float32)
    s = s * (MLA_NOPE + MLA_ROPE) ** -0.5
    p = masked_softmax(s, (k_pos[None, :] <= q_pos[:, None])[None, None])
    o_lat = jnp.einsum('bhqk,bkr->bqhr', p.astype(c.dtype), c)
    return jnp.einsum('bqhr,hrv->bqhv', o_lat, w_uv)


def even_project(x, pos, w_in, gate_b, q_norm_g, w_uq, kv_norm_g):
    b, t, _ = x.shape
    cuts = np.cumsum(EVEN_SPLITS)[:-1].tolist()
    q_nsa, kv_nsa, kv_win, gate, q_lat, kv_lat, k_rope = jnp.split(x @ w_in, cuts, axis=-1)
    q_nsa = q_nsa.reshape(b, t, NSA_HEADS, NSA_DK)
    kv_nsa = kv_nsa.reshape(b, t, 4, NSA_DK)
    kv_win = kv_win.reshape(b, t, 2, NSA_DK)
    gate = (gate + gate_b).reshape(b, t, 3, NSA_HEADS)
    q_mla = (rms_norm(q_lat, q_norm_g) @ w_uq).reshape(b, t, MLA_HEADS, MLA_NOPE + MLA_ROPE)
    q_nope = q_mla[..., :MLA_NOPE]
    q_rope = apply_rope(q_mla[..., MLA_NOPE:], pos)
    k_rope = apply_rope(k_rope[:, :, None, :], pos)[:, :, 0, :]
    mla_rows = jnp.concatenate([rms_norm(kv_lat, kv_norm_g), k_rope], axis=-1)
    return q_nsa, kv_nsa, kv_win, gate, q_nope, q_rope, mla_rows


def even_mixer_prompt(x, slopes, ep):
    w_in, gate_b, cmp_pe, cmp_w1, cmp_w2, q_norm_g, w_uq, kv_norm_g, w_uk, w_uv, w_out = ep
    b, s, _ = x.shape
    pos = jnp.arange(s, dtype=jnp.int32)
    q_nsa, kv_nsa, kv_win, gate, q_nope, q_rope, mla_rows = even_project(x, pos, w_in, gate_b, q_norm_g, w_uq, kv_norm_g)
    c_k, c_v, c_pos, s_k, s_v, cmap = nsa_keyside(kv_nsa, cmp_pe, cmp_w1, cmp_w2)
    win_pad = jnp.pad(kv_win, ((0, 0), (WINDOW, 0), (0, 0), (0, 0)))

    def block(q0):
        q_pos = q0 + jnp.arange(Q_BLOCK, dtype=jnp.int32)
        take = lambda a: lax.dynamic_slice_in_dim(a, q0, Q_BLOCK, axis=1)
        w_blk = lax.dynamic_slice_in_dim(win_pad, q0, WINDOW + Q_BLOCK, axis=1)
        w_pos = q0 - WINDOW + jnp.arange(WINDOW + Q_BLOCK, dtype=jnp.int32)
        o_nsa = nsa_core(take(q_nsa), take(gate), q_pos, c_k, c_v, c_pos, s_k, s_v,
                         w_blk[:, :, 0], w_blk[:, :, 1], w_pos, cmap, slopes)
        o_mla = mla_core(take(q_nope), take(q_rope), q_pos, mla_rows, pos, w_uk, w_uv)
        return jnp.concatenate([o_nsa.reshape(b, Q_BLOCK, -1), o_mla.reshape(b, Q_BLOCK, -1)], axis=-1)

    y = over_query_blocks(block, s) @ w_out
    return y, kv_nsa, kv_win[:, s - min(WINDOW, s):], mla_rows


def even_mixer_sample(x, nsa_past, win_buf, mla_past, slopes, ep):
    w_in, gate_b, cmp_pe, cmp_w1, cmp_w2, q_norm_g, w_uq, kv_norm_g, w_uk, w_uv, w_out = ep
    b, t, _ = x.shape
    past = nsa_past.shape[1]
    pos = past + jnp.arange(t, dtype=jnp.int32)
    q_nsa, kv_nsa, kv_win, gate, q_nope, q_rope, mla_rows = even_project(x, pos, w_in, gate_b, q_norm_g, w_uq, kv_norm_g)
    rows = jnp.concatenate([nsa_past, kv_nsa], axis=1)
    c_k, c_v, c_pos, s_k, s_v, cmap = nsa_keyside(rows, cmp_pe, cmp_w1, cmp_w2)
    w_all = jnp.concatenate([win_buf, kv_win], axis=1)
    n_buf = win_buf.shape[1]
    w_pos = past - n_buf + jnp.arange(n_buf + t, dtype=jnp.int32)
    o_nsa = nsa_core(q_nsa, gate, pos, c_k, c_v, c_pos, s_k, s_v, w_all[:, :, 0], w_all[:, :, 1], w_pos, cmap, slopes)
    mla_all = jnp.concatenate([mla_past, mla_rows], axis=1)
    o_mla = mla_core(q_nope, q_rope, pos, mla_all, jnp.arange(past + t, dtype=jnp.int32), w_uk, w_uv)
    y = jnp.concatenate([o_nsa.reshape(b, t, -1), o_mla.reshape(b, t, -1)], axis=-1) @ w_out
    return y, kv_nsa, w_all[:, t:], mla_rows


def diff_lambda_value(lam_p, lam_init):
    lp = lam_p.astype(jnp.float32)
    return jnp.exp(jnp.sum(lp[0] * lp[1])) - jnp.exp(jnp.sum(lp[2] * lp[3])) + lam_init


def diff_core(q, q_pos, k, v, k_pos, lam, lam_init, subln_g, slopes):
    b, tq = q.shape[:2]
    qg = q.reshape(b, tq, DIFF_KV_HEADS, DIFF_REP, 2, DIFF_D)
    s = jnp.einsum('bqgrmd,bkgmd->bmgrqk', qg, k).astype(jnp.float32) * DIFF_D ** -0.5
    dist = q_pos[:, None] - k_pos[None, :]
    s = s - slopes.reshape(DIFF_KV_HEADS, DIFF_REP)[None, None, :, :, None, None] * dist.astype(jnp.float32)
    p = masked_softmax(s, dist >= 0)
    a = p[:, 0] - lam * p[:, 1]
    o = jnp.einsum('bgrqk,bkge->bqgre', a.astype(v.dtype), v).reshape(b, tq, DIFF_HEADS, 2 * DIFF_D)
    o = rms_norm(o, subln_g) * (1.0 - lam_init)
    return o.reshape(b, tq, ODD_MIX)


def odd_project(x, w_in):
    b, t, _ = x.shape
    z = x @ w_in
    q = z[..., :ODD_MIX].reshape(b, t, DIFF_HEADS, 2, DIFF_D)
    kv = z[..., ODD_MIX:].reshape(b, t, 2, DIFF_KV_HEADS, 2 * DIFF_D)
    return q, kv


def odd_mixer_prompt(x, slopes, lam_init, op):
    w_in, lam_p, subln_g, w_out = op
    b, s, _ = x.shape
    q, kv = odd_project(x, w_in)
    k = kv[:, :, 0].reshape(b, s, DIFF_KV_HEADS, 2, DIFF_D)
    v = kv[:, :, 1]
    lam = diff_lambda_value(lam_p, lam_init)
    pos = jnp.arange(s, dtype=jnp.int32)

    def block(q0):
        qb = lax.dynamic_slice_in_dim(q, q0, Q_BLOCK, axis=1)
        return diff_core(qb, q0 + jnp.arange(Q_BLOCK, dtype=jnp.int32), k, v, pos, lam, lam_init, subln_g, slopes)

    return over_query_blocks(block, s) @ w_out, kv


def odd_mixer_sample(x, kv_past, slopes, lam_init, op):
    w_in, lam_p, subln_g, w_out = op
    b, t, _ = x.shape
    past = kv_past.shape[1]
    q, kv = odd_project(x, w_in)
    kv_all = jnp.concatenate([kv_past, kv], axis=1)
    k = kv_all[:, :, 0].reshape(b, past + t, DIFF_KV_HEADS, 2, DIFF_D)
    v = kv_all[:, :, 1]
    lam = diff_lambda_value(lam_p, lam_init)
    o = diff_core(q, past + jnp.arange(t, dtype=jnp.int32), k, v, jnp.arange(past + t, dtype=jnp.int32),
                  lam, lam_init, subln_g, slopes)
    return o @ w_out, kv


def setup_inputs(seed: int = 0) -> dict:
    key = jax.random.key(seed)
    ks = jax.random.split(key, 32)
    n_pages = PAST_LEN // PAGE_SIZE
    n_used = DEC_BATCH * n_pages
    n_pool = (5 * n_used + 3) // 4
    wbuf = min(WINDOW, PAST_LEN)

    def nrm(k, shape, scale):
        return jax.random.normal(k, shape, jnp.float32) * scale

    return {
        'x_prompt': nrm(ks[0], (BATCH, SEQ, D_MODEL), 1.0),
        'x_sample': nrm(ks[1], (DEC_BATCH, DEC_SEQ, D_MODEL), 1.0),
        'cache_nsa': nrm(ks[2], (N_EVEN, n_pool, PAGE_SIZE, 4, NSA_DK), 1.0),
        'cache_nsa_win': nrm(ks[3], (N_EVEN, DEC_BATCH, wbuf, 2, NSA_DK), 1.0),
        'cache_mla': nrm(ks[4], (N_EVEN, n_pool, PAGE_SIZE, MLA_KV_RANK + MLA_ROPE), 1.0),
        'cache_diff': nrm(ks[5], (N_ODD, n_pool, PAGE_SIZE, 2, DIFF_KV_HEADS, 2 * DIFF_D), 1.0),
        'page_table': jax.random.permutation(ks[6], n_pool)[:n_used].astype(jnp.int32).reshape(DEC_BATCH, n_pages),
        'ln_g': 1.0 + nrm(ks[7], (DEPTH, 3, D_MODEL), 0.02),
        'ln_b': nrm(ks[8], (DEPTH, 3, D_MODEL), 0.02),
        'ffn_w_in': nrm(ks[9], (DEPTH, 2, D_MODEL, 2 * FFN_HIDDEN), D_MODEL ** -0.5),
        'ffn_w_out': nrm(ks[10], (DEPTH, 2, FFN_HIDDEN, D_MODEL), FFN_HIDDEN ** -0.5 * BETA),
        'even_w_in': nrm(ks[11], (N_EVEN, D_MODEL, EVEN_IN), D_MODEL ** -0.5),
        'nsa_gate_b': nrm(ks[12], (N_EVEN, 3 * NSA_HEADS), 0.1),
        'nsa_cmp_pe': nrm(ks[13], (N_EVEN, 2, CMP_LEN, NSA_DK), 0.02),
        'nsa_cmp_w1': nrm(ks[14], (N_EVEN, 2, CMP_LEN, NSA_DK, CMP_HIDDEN), (CMP_LEN * NSA_DK) ** -0.5),
        'nsa_cmp_w2': nrm(ks[15], (N_EVEN, 2, CMP_HIDDEN, NSA_DK), CMP_HIDDEN ** -0.5),
        'mla_q_norm_g': 1.0 + nrm(ks[16], (N_EVEN, MLA_Q_RANK), 0.02),
        'mla_w_uq': nrm(ks[17], (N_EVEN, MLA_Q_RANK, MLA_HEADS * (MLA_NOPE + MLA_ROPE)), MLA_Q_RANK ** -0.5),
        'mla_kv_norm_g': 1.0 + nrm(ks[18], (N_EVEN, MLA_KV_RANK), 0.02),
        'mla_w_uk': nrm(ks[19], (N_EVEN, MLA_HEADS, MLA_KV_RANK, MLA_NOPE), MLA_KV_RANK ** -0.5),
        'mla_w_uv': nrm(ks[20], (N_EVEN, MLA_HEADS, MLA_KV_RANK, MLA_V), MLA_KV_RANK ** -0.5),
        'even_w_out': nrm(ks[21], (N_EVEN, EVEN_MIX, D_MODEL), EVEN_MIX ** -0.5 * BETA),
        'odd_w_in': nrm(ks[22], (N_ODD, D_MODEL, ODD_IN), D_MODEL ** -0.5),
        'diff_lambda': nrm(ks[23], (N_ODD, 4, DIFF_D), 0.1),
        'diff_subln_g': 1.0 + nrm(ks[24], (N_ODD, 2 * DIFF_D), 0.02),
        'odd_w_out': nrm(ks[25], (N_ODD, ODD_MIX, D_MODEL), ODD_MIX ** -0.5 * BETA),
    }


def reference(x_prompt, x_sample, cache_nsa, cache_nsa_win, cache_mla, cache_diff, page_table,
              ln_g, ln_b, ffn_w_in, ffn_w_out,
              even_w_in, nsa_gate_b, nsa_cmp_pe, nsa_cmp_w1, nsa_cmp_w2,
              mla_q_norm_g, mla_w_uq, mla_kv_norm_g, mla_w_uk, mla_w_uv, even_w_out,
              odd_w_in, diff_lambda, diff_subln_g, odd_w_out):
    nsa_slopes = alibi_slopes(NSA_HEADS)
    diff_slopes = alibi_slopes(DIFF_HEADS)
    n_seq = page_table.shape[0]
    past = page_table.shape[1] * PAGE_SIZE
    xp, xs = x_prompt, x_sample
    nsa_p, nsa_s, win_p, win_s, mla_p, mla_s, diff_p, diff_s = [], [], [], [], [], [], [], []
    for li in range(DEPTH):
        j = li // 2
        xp = layer_norm(ALPHA * xp + half_swiglu(xp, ffn_w_in[li, 0], ffn_w_out[li, 0]), ln_g[li, 0], ln_b[li, 0])
        xs = layer_norm(ALPHA * xs + half_swiglu(xs, ffn_w_in[li, 0], ffn_w_out[li, 0]), ln_g[li, 0], ln_b[li, 0])
        if li % 2 == 0:
            ep = (even_w_in[j], nsa_gate_b[j], nsa_cmp_pe[j], nsa_cmp_w1[j], nsa_cmp_w2[j],
                  mla_q_norm_g[j], mla_w_uq[j], mla_kv_norm_g[j], mla_w_uk[j], mla_w_uv[j], even_w_out[j])
            yp, r_nsa, r_win, r_mla = even_mixer_prompt(xp, nsa_slopes, ep)
            nsa_past = cache_nsa[j, page_table].reshape(n_seq, past, 4, NSA_DK)
            mla_past = cache_mla[j, page_table].reshape(n_seq, past, MLA_KV_RANK + MLA_ROPE)
            ys, q_nsa, q_win, q_mla = even_mixer_sample(xs, nsa_past, cache_nsa_win[j], mla_past, nsa_slopes, ep)
            nsa_p.append(r_nsa); win_p.append(r_win); mla_p.append(r_mla)
            nsa_s.append(q_nsa); win_s.append(q_win); mla_s.append(q_mla)
        else:
            lam_init = 0.8 - 0.6 * math.exp(-0.3 * li)
            op = (odd_w_in[j], diff_lambda[j], diff_subln_g[j], odd_w_out[j])
            yp, r_kv = odd_mixer_prompt(xp, diff_slopes, lam_init, op)
            kv_past = cache_diff[j, page_table].reshape(n_seq, past, 2, DIFF_KV_HEADS, 2 * DIFF_D)
            ys, q_kv = odd_mixer_sample(xs, kv_past, diff_slopes, lam_init, op)
            diff_p.append(r_kv); diff_s.append(q_kv)
        xp = layer_norm(ALPHA * xp + yp, ln_g[li, 1], ln_b[li, 1])
        xs = layer_norm(ALPHA * xs + ys, ln_g[li, 1], ln_b[li, 1])
        xp = layer_norm(ALPHA * xp + half_swiglu(xp, ffn_w_in[li, 1], ffn_w_out[li, 1]), ln_g[li, 2], ln_b[li, 2])
        xs = layer_norm(ALPHA * xs + half_swiglu(xs, ffn_w_in[li, 1], ffn_w_out[li, 1]), ln_g[li, 2], ln_b[li, 2])
    new_nsa_prompt = jnp.stack(nsa_p, 0)
    new_nsa_sample = jnp.stack(nsa_s, 0)
    new_win_prompt = jnp.stack(win_p, 0)
    new_win_sample = jnp.stack(win_s, 0)
    new_mla_prompt = jnp.stack(mla_p, 0)
    new_mla_sample = jnp.stack(mla_s, 0)
    new_diff_prompt = jnp.stack(diff_p, 0)
    new_diff_sample = jnp.stack(diff_s, 0)
    return (xp, xs, new_nsa_prompt, new_nsa_sample, new_win_prompt, new_win_sample,
            new_mla_prompt, new_mla_sample, new_diff_prompt, new_diff_sample)
```

```python
import functools
import math

import numpy as np
import jax
import jax.numpy as jnp
from jax import lax
from jax.experimental import pallas as pl
from jax.experimental.pallas import tpu as pltpu

D_MODEL = 1024
DEPTH = 4
PAGE_SIZE = 128
ALPHA = (2.0 * DEPTH) ** 0.25
LN_EPS = 1e-5
RMS_EPS = 1e-6
FFN_HIDDEN = ((8 * D_MODEL // 3 + 127) // 128) * 128
NSA_HEADS = 8
NSA_DK = 64
CMP_LEN = 32
CMP_STRIDE = 16
CMP_HIDDEN = 2 * NSA_DK
SLC_LEN = 64
N_SELECT = 16
WINDOW = 512
FORCED_SCORE = 1e9
MLA_HEADS = 8
MLA_Q_RANK = 256
MLA_KV_RANK = 128
MLA_NOPE = 64
MLA_ROPE = 32
MLA_V = 64
ROPE_THETA = 10000.0
DIFF_HEADS = 8
DIFF_KV_HEADS = 2
DIFF_D = 64
DIFF_REP = DIFF_HEADS // DIFF_KV_HEADS
EVEN_SPLITS = (NSA_HEADS * NSA_DK, 4 * NSA_DK, 2 * NSA_DK, 3 * NSA_HEADS, MLA_Q_RANK, MLA_KV_RANK, MLA_ROPE)
ODD_MIX = DIFF_HEADS * 2 * DIFF_D

LANES = 128
MASKED = -1e30
M_INIT = -1e29
VMEM_LIMIT_BYTES = 56 * 2 ** 20
EVEN_IN_PAD = 1408
GATE_LANE0 = 32
BF16 = jnp.bfloat16
F32 = jnp.float32


def _cparams(*sem):
    return pltpu.CompilerParams(dimension_semantics=sem, vmem_limit_bytes=VMEM_LIMIT_BYTES)


def _pick_tile(n, cap, mult=8):
    t = min(cap, n)
    while t > mult and (n % t or t % mult):
        t -= mult
    assert n % t == 0, (n, cap, mult)
    return t


def _layer_norm(y, g, b):
    mu = jnp.mean(y, axis=-1, keepdims=True)
    d = y - mu
    var = jnp.mean(d * d, axis=-1, keepdims=True)
    return d * lax.rsqrt(var + LN_EPS) * g + b


def _rms_norm(x, g):
    return x * lax.rsqrt(jnp.mean(x * x, axis=-1, keepdims=True) + RMS_EPS) * g


def _dot(a, b):
    return jnp.dot(a, b, preferred_element_type=F32)


def _dot_nt(a, b):
    return lax.dot_general(a, b, (((1,), (1,)), ((), ())), preferred_element_type=F32)


def _ffn_ln_body(x_ref, wg_ref, wu_ref, wo_ref, g_ref, b_ref, o_ref, acc_ref, xb_ref, *, nh):
    h = pl.program_id(1)

    @pl.when(h == 0)
    def _():
        acc_ref[...] = jnp.zeros_like(acc_ref)
        xb_ref[...] = x_ref[...].astype(BF16)

    xb = xb_ref[...]
    gate = _dot(xb, wg_ref[...])
    up = _dot(xb, wu_ref[...])
    act = (gate * jax.nn.sigmoid(gate) * up).astype(BF16)
    acc_ref[...] += _dot(act, wo_ref[...])

    @pl.when(h == nh - 1)
    def _():
        y = ALPHA * x_ref[...] + 0.5 * acc_ref[...]
        o_ref[...] = _layer_norm(y, g_ref[...], b_ref[...])


def ffn_ln(x, w_in, w_out, g, b):
    m = x.shape[0]
    tm = _pick_tile(m, 1024)
    th = 256
    nh = FFN_HIDDEN // th
    return pl.pallas_call(
        functools.partial(_ffn_ln_body, nh=nh),
        grid=(m // tm, nh),
        in_specs=[
            pl.BlockSpec((tm, D_MODEL), lambda i, h: (i, 0)),
            pl.BlockSpec((D_MODEL, th), lambda i, h: (0, h)),
            pl.BlockSpec((D_MODEL, th), lambda i, h: (0, h + nh)),
            pl.BlockSpec((th, D_MODEL), lambda i, h: (h, 0)),
            pl.BlockSpec((1, D_MODEL), lambda i, h: (0, 0)),
            pl.BlockSpec((1, D_MODEL), lambda i, h: (0, 0)),
        ],
        out_specs=pl.BlockSpec((tm, D_MODEL), lambda i, h: (i, 0)),
        out_shape=jax.ShapeDtypeStruct((m, D_MODEL), F32),
        scratch_shapes=[pltpu.VMEM((tm, D_MODEL), F32), pltpu.VMEM((tm, D_MODEL), BF16)],
        compiler_params=_cparams("parallel", "arbitrary"),
        name="ffn_ln",
    )(x, w_in, w_in, w_out, g.reshape(1, -1), b.reshape(1, -1))


def _mm_body(x_ref, w_ref, o_ref):
    o_ref[...] = _dot(x_ref[...], w_ref[...])


def matmul(x, w):
    m, k = x.shape
    n = w.shape[1]
    tm = _pick_tile(m, 512, 16)
    return pl.pallas_call(
        _mm_body,
        grid=(m // tm,),
        in_specs=[pl.BlockSpec((tm, k), lambda i: (i, 0)), pl.BlockSpec((k, n), lambda i: (0, 0))],
        out_specs=pl.BlockSpec((tm, n), lambda i: (i, 0)),
        out_shape=jax.ShapeDtypeStruct((m, n), F32),
        compiler_params=_cparams("parallel"),
        name="cmp_matmul",
    )(x, w)


def _even_proj_body(x_ref, w_ref, cos_ref, sin_ref, gb_ref, gq_ref, gkv_ref, wuq_ref, wuk_ref,
                    kvn_ref, kvw_ref, gate_ref, qn_ref, rows_ref, mk_ref, qm_ref):
    tm = x_ref.shape[0]
    z = _dot(x_ref[...].astype(BF16), w_ref[...])
    kvn_ref[...] = z[:, 512:768]
    kvw_ref[...] = z[:, 768:896]
    lane = lax.broadcasted_iota(jnp.int32, (tm, LANES), 1)
    low_half = lane < 64
    nsa_scale = NSA_DK ** -0.5
    for h in range(NSA_HEADS):
        grp = z[:, 128 * (h // 2):128 * (h // 2) + 128]
        if h % 2:
            grp = pltpu.roll(grp, 64, 1)
        qn_ref[h] = jnp.where(low_half, grp * nsa_scale, 0.0)
    slab = z[:, 1280:1408]
    gate_ref[...] = jax.nn.sigmoid(slab + gb_ref[...])
    cos = cos_ref[...]
    sin = sin_ref[...]
    first_half = (lane & 31) < 16

    def rope(v):
        swapped = jnp.where(first_half, pltpu.roll(v, LANES - 16, 1), pltpu.roll(v, 16, 1))
        return v * cos + swapped * sin

    kr = rope(slab)
    c = _rms_norm(z[:, 1152:1280], gkv_ref[...])
    rows_ref[:, 0:128] = c
    rows_ref[:, 128:160] = kr[:, 0:32]
    t = jnp.where(lane < 32, kr, 0.0)
    t = t + pltpu.roll(t, 32, 1)
    t = t + pltpu.roll(t, 64, 1)
    mk_ref[:, 0:128] = c
    mk_ref[:, 128:256] = t
    ql = _rms_norm(z[:, 896:1152], gq_ref[...])
    qm = _dot(ql.astype(BF16), wuq_ref[...])
    qabs = _dot(qm[:, 0:512].astype(BF16), wuk_ref[...])
    r0 = rope(qm[:, 512:640])
    r1 = rope(qm[:, 640:768])
    mla_scale = (MLA_NOPE + MLA_ROPE) ** -0.5
    grp32 = lane >> 5
    for h in range(MLA_HEADS):
        qm_ref[h, :, 0:128] = qabs[:, 128 * h:128 * h + 128] * mla_scale
        rr = r0 if h < 4 else r1
        qm_ref[h, :, 128:256] = jnp.where(grp32 == (h % 4), rr * mla_scale, 0.0)


def even_proj(x, w, cos, sin, gb, gq, gkv, wuq, wuk):
    m = x.shape[0]
    tm = _pick_tile(m, 512)
    row = lambda i: (i, 0)
    full2 = lambda i: (0, 0)
    out_shapes = [
        jax.ShapeDtypeStruct((m, 256), F32),
        jax.ShapeDtypeStruct((m, 128), F32),
        jax.ShapeDtypeStruct((m, 128), F32),
        jax.ShapeDtypeStruct((NSA_HEADS, m, 128), F32),
        jax.ShapeDtypeStruct((m, 160), F32),
        jax.ShapeDtypeStruct((m, 256), F32),
        jax.ShapeDtypeStruct((MLA_HEADS, m, 256), F32),
    ]
    out_specs = [
        pl.BlockSpec((tm, 256), row), pl.BlockSpec((tm, 128), row), pl.BlockSpec((tm, 128), row),
        pl.BlockSpec((NSA_HEADS, tm, 128), lambda i: (0, i, 0)),
        pl.BlockSpec((tm, 160), row), pl.BlockSpec((tm, 256), row),
        pl.BlockSpec((MLA_HEADS, tm, 256), lambda i: (0, i, 0)),
    ]
    return pl.pallas_call(
        _even_proj_body,
        grid=(m // tm,),
        in_specs=[
            pl.BlockSpec((tm, D_MODEL), row),
            pl.BlockSpec((D_MODEL, EVEN_IN_PAD), full2),
            pl.BlockSpec((tm, 128), row), pl.BlockSpec((tm, 128), row),
            pl.BlockSpec((1, 128), full2), pl.BlockSpec((1, 256), full2), pl.BlockSpec((1, 128), full2),
            pl.BlockSpec((MLA_Q_RANK, 768), full2),
            pl.BlockSpec((512, 1024), full2),
        ],
        out_specs=out_specs,
        out_shape=out_shapes,
        compiler_params=_cparams("parallel"),
        name="even_proj",
    )(x, w, cos, sin, gb, gq, gkv, wuq, wuk)


def _odd_proj_body(x_ref, w_ref, kv_ref, q_ref):
    tm = x_ref.shape[0]
    z = _dot(x_ref[...].astype(BF16), w_ref[...])
    kv_ref[...] = z[:, ODD_MIX:]
    lane = lax.broadcasted_iota(jnp.int32, (tm, LANES), 1)
    scale = DIFF_D ** -0.5
    for g in range(DIFF_KV_HEADS):
        for mm in range(2):
            keep = (lane < 64) if mm == 0 else (lane >= 64)
            for r in range(DIFF_REP):
                hd = g * DIFF_REP + r
                q_ref[g, mm * DIFF_REP + r] = jnp.where(keep, z[:, 128 * hd:128 * hd + 128] * scale, 0.0)


def odd_proj(x, w):
    m = x.shape[0]
    tm = _pick_tile(m, 512)
    return pl.pallas_call(
        _odd_proj_body,
        grid=(m // tm,),
        in_specs=[pl.BlockSpec((tm, D_MODEL), lambda i: (i, 0)),
                  pl.BlockSpec((D_MODEL, w.shape[1]), lambda i: (0, 0))],
        out_specs=[pl.BlockSpec((tm, 512), lambda i: (i, 0)),
                   pl.BlockSpec((DIFF_KV_HEADS, 2 * DIFF_REP, tm, 128), lambda i: (0, 0, i, 0))],
        out_shape=[jax.ShapeDtypeStruct((m, 512), F32),
                   jax.ShapeDtypeStruct((DIFF_KV_HEADS, 2 * DIFF_REP, m, 128), F32)],
        compiler_params=_cparams("parallel"),
        name="odd_proj",
    )(x, w)


def _even_out_body(x_ref, oc_ref, os_ref, ow_ref, om_ref, wn_ref, wm_ref, g_ref, b_ref, o_ref):
    y = _dot(om_ref[...].astype(BF16), wm_ref[...])
    for h in range(NSA_HEADS):
        y = y + _dot((oc_ref[h] + os_ref[h] + ow_ref[h]).astype(BF16), wn_ref[h])
    o_ref[...] = _layer_norm(ALPHA * x_ref[...] + y, g_ref[...], b_ref[...])


def even_out(x, oc, osl, ow, om, wn, wm, g, b):
    m = x.shape[0]
    tm = _pick_tile(m, 512)
    row = lambda i: (i, 0)
    hspec = pl.BlockSpec((NSA_HEADS, tm, 128), lambda i: (0, i, 0))
    return pl.pallas_call(
        _even_out_body,
        grid=(m // tm,),
        in_specs=[pl.BlockSpec((tm, D_MODEL), row), hspec, hspec, hspec,
                  pl.BlockSpec((tm, 512), row),
                  pl.BlockSpec((NSA_HEADS, 128, D_MODEL), lambda i: (0, 0, 0)),
                  pl.BlockSpec((512, D_MODEL), lambda i: (0, 0)),
                  pl.BlockSpec((1, D_MODEL), lambda i: (0, 0)),
                  pl.BlockSpec((1, D_MODEL), lambda i: (0, 0))],
        out_specs=pl.BlockSpec((tm, D_MODEL), row),
        out_shape=jax.ShapeDtypeStruct((m, D_MODEL), F32),
        compiler_params=_cparams("parallel"),
        name="even_out",
    )(x, oc, osl, ow, om, wn, wm, g.reshape(1, -1), b.reshape(1, -1))


def _odd_out_body(x_ref, y_ref, w_ref, g_ref, b_ref, o_ref):
    y = _dot(y_ref[...].astype(BF16), w_ref[...])
    o_ref[...] = _layer_norm(ALPHA * x_ref[...] + y, g_ref[...], b_ref[...])


def odd_out(x, y, w, g, b):
    m = x.shape[0]
    tm = _pick_tile(m, 512)
    row = lambda i: (i, 0)
    return pl.pallas_call(
        _odd_out_body,
        grid=(m // tm,),
        in_specs=[pl.BlockSpec((tm, D_MODEL), row), pl.BlockSpec((tm, ODD_MIX), row),
                  pl.BlockSpec((ODD_MIX, D_MODEL), lambda i: (0, 0)),
                  pl.BlockSpec((1, D_MODEL), lambda i: (0, 0)),
                  pl.BlockSpec((1, D_MODEL), lambda i: (0, 0))],
        out_specs=pl.BlockSpec((tm, D_MODEL), row),
        out_shape=jax.ShapeDtypeStruct((m, D_MODEL), F32),
        compiler_params=_cparams("parallel"),
        name="odd_out",
    )(x, y, w, g.reshape(1, -1), b.reshape(1, -1))


def _cmp_combine_body(pt_ref, *refs, n_pages):
    f_refs = refs[:n_pages]
    pe_ref, w1_ref, w2k_ref, w2v_ref, o_ref = refs[n_pages:]
    f = jnp.concatenate([r[0] for r in f_refs], axis=0)
    n = f.shape[0]
    slab = jnp.zeros((n, 128), F32)
    for part, w2_ref in ((0, w2k_ref), (1, w2v_ref)):
        first = f[:, 256 * part:256 * part + 128]
        second = f[:, 256 * part + 128:256 * part + 256]
        pe_term = _dot(pe_ref[part], w1_ref[part])[0:1]
        hid = first + pltpu.roll(second, n - 1, 0) + pe_term
        slab = slab + _dot(jax.nn.gelu(hid).astype(BF16), w2_ref[...])
    o_ref[0] = slab.astype(BF16)


def cmp_combine(f_pages, table, pe, w1, w2k, w2v):
    n_seq, n_pages = table.shape
    idx = lambda i, s, pt: (pt[s * n_pages + i], 0, 0)
    full3 = lambda s, pt: (0, 0, 0)
    grid_spec = pltpu.PrefetchScalarGridSpec(
        num_scalar_prefetch=1,
        grid=(n_seq,),
        in_specs=[pl.BlockSpec((1, 8, 512), functools.partial(idx, i)) for i in range(n_pages)] + [
            pl.BlockSpec((2, 8, 2048), full3), pl.BlockSpec((2, 2048, 128), full3),
            pl.BlockSpec((128, 128), lambda s, pt: (0, 0)), pl.BlockSpec((128, 128), lambda s, pt: (0, 0))],
        out_specs=pl.BlockSpec((1, 8 * n_pages, 128), lambda s, pt: (s, 0, 0)),
    )
    return pl.pallas_call(
        functools.partial(_cmp_combine_body, n_pages=n_pages),
        grid_spec=grid_spec,
        out_shape=jax.ShapeDtypeStruct((n_seq, 8 * n_pages, 128), BF16),
        compiler_params=_cparams("arbitrary"),
        name="cmp_combine",
    )(table.reshape(-1), *([f_pages] * n_pages), pe, w1, w2k, w2v)


def _softmax_step(s_all, mask, distf, slopes, m_ref, l_ref, a_ref, p_ref, n_groups, tq):
    tk = s_all.shape[1]
    for r in range(n_groups):
        rows = slice(r * tq, (r + 1) * tq)
        s = s_all[rows]
        if slopes is not None:
            s = s - slopes[r] * distf
        s = jnp.where(mask, s, MASKED)
        m_prev = m_ref[rows]
        m_new = jnp.maximum(m_prev, jnp.max(s, axis=1, keepdims=True))
        a = jnp.exp(m_prev - m_new)
        p = jnp.exp(s - m_new)
        l_ref[rows] = a * l_ref[rows] + jnp.sum(p, axis=1, keepdims=True)
        m_ref[rows] = m_new
        a_ref[rows] = a
        p_ref[rows, 0:tk] = p.astype(BF16)


def _inv_or_zero(l):
    return jnp.where(l > 0.0, 1.0 / l, 0.0)


def _finish_mla(acc, l, wuv_ref, o_ref, tq):
    ol = (acc * _inv_or_zero(l)).astype(BF16)
    out = _dot(ol[0:tq], wuv_ref[0])
    for h in range(1, MLA_HEADS):
        out = out + _dot(ol[h * tq:(h + 1) * tq], wuv_ref[h])
    o_ref[...] = out


def _finish_nsa(acc, l, gate, branch, o_ref, tq):
    o = acc * _inv_or_zero(l)
    for h in range(NSA_HEADS):
        c = GATE_LANE0 + NSA_HEADS * branch + h
        o_ref[h] = o[h * tq:(h + 1) * tq] * gate[:, c:c + 1]


def _finish_diff(acc, l, lam_ref, sub_ref, lam_init, o_ref, tq):
    lp = lam_ref[...]
    lam = (jnp.exp(jnp.sum(lp[0:1] * lp[1:2], axis=1, keepdims=True))
           - jnp.exp(jnp.sum(lp[2:3] * lp[3:4], axis=1, keepdims=True)) + lam_init)
    o = acc * _inv_or_zero(l)
    half = DIFF_REP * tq
    d = o[0:half] - lam * o[half:2 * half]
    d = _rms_norm(d, sub_ref[...]) * (1.0 - lam_init)
    for r in range(DIFF_REP):
        o_ref[:, 128 * r:128 * r + 128] = d[r * tq:(r + 1) * tq]


def _alibi(n_heads, h):
    return 2.0 ** (-8.0 * (h + 1) / n_heads)


def _prompt_flash_body(qi_ref, kj_ref, fl_ref, ll_ref, *refs, mode, tq, tk, lam_init):
    refs = list(refs)
    q_ref = refs.pop(0)
    k_ref = refs.pop(0)
    v_ref = refs.pop(0) if mode == "diff" else None
    if mode == "slc":
        sel_ref = refs.pop(0)
        e_ref = refs.pop(0)
    if mode in ("slc", "win"):
        gate_ref = refs.pop(0)
    if mode == "mla":
        wuv_ref = refs.pop(0)
    if mode == "diff":
        lam_ref = refs.pop(0)
        sub_ref = refs.pop(0)
    o_ref, qb_ref, m_ref, l_ref, a_ref, acc_ref, p_ref = refs
    n_groups = q_ref.shape[1]
    g = pl.program_id(1)
    s = pl.program_id(2)

    @pl.when(fl_ref[s] == 1)
    def _():
        qb_ref[...] = q_ref[0].reshape(n_groups * tq, q_ref.shape[3]).astype(BF16)
        m_ref[...] = jnp.full_like(m_ref, M_INIT)
        l_ref[...] = jnp.zeros_like(l_ref)
        acc_ref[...] = jnp.zeros_like(acc_ref)

    kb = k_ref[...].astype(BF16)
    s_all = _dot_nt(qb_ref[...], kb)
    q_pos = qi_ref[s] * tq + lax.broadcasted_iota(jnp.int32, (tq, tk), 0)
    k_pos = kj_ref[s] * tk + lax.broadcasted_iota(jnp.int32, (tq, tk), 1)
    dist = q_pos - k_pos
    mask = dist >= 0
    if mode == "win":
        mask = jnp.logical_and(mask, dist <= WINDOW)
    if mode == "slc":
        picked = _dot(sel_ref[...].astype(BF16), e_ref[...])
        mask = jnp.logical_and(mask, picked > 0.5)
    if mode == "mla":
        slopes = None
    elif mode == "diff":
        s0 = [_alibi(DIFF_HEADS, r) for r in range(DIFF_REP)] * 2
        s1 = [_alibi(DIFF_HEADS, DIFF_REP + r) for r in range(DIFF_REP)] * 2
        slopes = [jnp.where(g == 0, a, b) for a, b in zip(s0, s1)]
    else:
        slopes = [_alibi(NSA_HEADS, h) for h in range(NSA_HEADS)]
    _softmax_step(s_all, mask, dist.astype(F32), slopes, m_ref, l_ref, a_ref, p_ref, n_groups, tq)
    vb = kb[:, 0:128] if mode != "diff" else v_ref[...].astype(BF16)
    acc_ref[...] = a_ref[...] * acc_ref[...] + _dot(p_ref[...], vb)

    @pl.when(ll_ref[s] == 1)
    def _():
        if mode == "mla":
            _finish_mla(acc_ref[...], l_ref[...], wuv_ref, o_ref, tq)
        elif mode == "diff":
            _finish_diff(acc_ref[...], l_ref[...], lam_ref, sub_ref, lam_init, o_ref, tq)
        else:
            _finish_nsa(acc_ref[...], l_ref[...], gate_ref[...], 1 if mode == "slc" else 2, o_ref, tq)


def _prompt_steps(seq, tq, tk, mode):
    qi, kj, fl, ll = [], [], [], []
    for i in range(seq // tq):
        hi = (i * tq + tq - 1) // tk
        lo = max(0, (i * tq - WINDOW) // tk) if mode == "win" else 0
        for j in range(lo, hi + 1):
            qi.append(i); kj.append(j); fl.append(int(j == lo)); ll.append(int(j == hi))
    return [jnp.asarray(np.asarray(a, np.int32)) for a in (qi, kj, fl, ll)]


def prompt_flash(mode, batch, seq, q, k, v=None, sel=None, emat=None, gate=None, wuv=None,
                 lam=None, sub=None, lam_init=0.0):
    n_g, n_groups, m, dk = q.shape
    tq, tk = (256, 256) if mode == "win" else (128, 512)
    tq, tk = min(tq, seq), min(tk, seq)
    nq, nk = seq // tq, seq // tk
    steps = _prompt_steps(seq, tq, tk, mode)
    n_steps = int(steps[0].shape[0])
    qrow = lambda b, g, s, qi, kj, fl, ll: (b * nq + qi[s], 0)
    kcol = {"mla": 0, "slc": 1, "win": 0, "diff": None}[mode]
    kw = 256 if mode == "mla" else 128
    if mode == "diff":
        kmap = lambda b, g, s, qi, kj, fl, ll: (b * nk + kj[s], g)
    else:
        kmap = lambda b, g, s, qi, kj, fl, ll: (b * nk + kj[s], kcol)
    in_specs = [pl.BlockSpec((1, n_groups, tq, dk), lambda b, g, s, qi, kj, fl, ll: (g, 0, b * nq + qi[s], 0)),
                pl.BlockSpec((tk, kw), kmap)]
    args = [q, k]
    if mode == "diff":
        in_specs.append(pl.BlockSpec((tk, 128), lambda b, g, s, qi, kj, fl, ll: (b * nk + kj[s], 2 + g)))
        args.append(k)
    if mode == "slc":
        in_specs += [pl.BlockSpec((tq, 128), qrow),
                     pl.BlockSpec((128, tk), lambda b, g, s, qi, kj, fl, ll: (0, kj[s]))]
        args += [sel, emat]
    if mode in ("slc", "win"):
        in_specs.append(pl.BlockSpec((tq, 128), qrow))
        args.append(gate)
    if mode == "mla":
        in_specs.append(pl.BlockSpec((MLA_HEADS, 128, 512), lambda b, g, s, qi, kj, fl, ll: (0, 0, 0)))
        args.append(wuv)
    if mode == "diff":
        in_specs += [pl.BlockSpec((4, DIFF_D), lambda b, g, s, qi, kj, fl, ll: (0, 0)),
                     pl.BlockSpec((1, 128), lambda b, g, s, qi, kj, fl, ll: (0, 0))]
        args += [lam, sub]
    rows = batch * seq
    if mode == "mla":
        out_spec = pl.BlockSpec((tq, 512), qrow)
        out_shape = jax.ShapeDtypeStruct((rows, 512), F32)
    elif mode == "diff":
        out_spec = pl.BlockSpec((tq, 512), lambda b, g, s, qi, kj, fl, ll: (b * nq + qi[s], g))
        out_shape = jax.ShapeDtypeStruct((rows, ODD_MIX), F32)
    else:
        out_spec = pl.BlockSpec((NSA_HEADS, tq, 128), lambda b, g, s, qi, kj, fl, ll: (0, b * nq + qi[s], 0))
        out_shape = jax.ShapeDtypeStruct((NSA_HEADS, rows, 128), F32)
    nr = n_groups * tq
    grid_spec = pltpu.PrefetchScalarGridSpec(
        num_scalar_prefetch=4,
        grid=(batch, n_g, n_steps),
        in_specs=in_specs,
        out_specs=out_spec,
        scratch_shapes=[pltpu.VMEM((nr, dk), BF16), pltpu.VMEM((nr, 1), F32), pltpu.VMEM((nr, 1), F32),
                        pltpu.VMEM((nr, 1), F32), pltpu.VMEM((nr, 128), F32), pltpu.VMEM((nr, tk), BF16)],
    )
    return pl.pallas_call(
        functools.partial(_prompt_flash_body, mode=mode, tq=tq, tk=tk, lam_init=lam_init),
        grid_spec=grid_spec,
        out_shape=out_shape,
        compiler_params=_cparams("parallel", "parallel", "arbitrary"),
        name="prompt_flash_" + mode,
    )(*steps, *args)


def _cmp_select_body(q_ref, c_ref, cmap_ref, gate_ref, oc_ref, sel_ref, *, tq, n_cmp, pos0_fn, k_eff):
    nc = c_ref.shape[1]
    qb = q_ref[0].reshape(NSA_HEADS * tq, 128).astype(BF16)
    cb = c_ref[0]
    s_all = _dot_nt(qb, cb)
    pos0 = pos0_fn()
    col = lax.broadcasted_iota(jnp.int32, (tq, nc), 1)
    q_pos = pos0 + lax.broadcasted_iota(jnp.int32, (tq, nc), 0)
    dist = q_pos - (col * CMP_STRIDE + (CMP_LEN - 1))
    mask = jnp.logical_and(dist >= 0, col < n_cmp)
    distf = dist.astype(F32)
    gate = gate_ref[...]
    psum = jnp.zeros((tq, nc), F32)
    for h in range(NSA_HEADS):
        s = s_all[h * tq:(h + 1) * tq] - _alibi(NSA_HEADS, h) * distf
        s = jnp.where(mask, s, MASKED)
        e = jnp.where(mask, jnp.exp(s - jnp.max(s, axis=1, keepdims=True)), 0.0)
        p = e * _inv_or_zero(jnp.sum(e, axis=1, keepdims=True))
        psum = psum + p
        c = GATE_LANE0 + h
        oc_ref[h] = _dot(p.astype(BF16), cb) * gate[:, c:c + 1]
    cmap = cmap_ref[...]
    hi = psum.astype(BF16)
    rem = psum - hi.astype(F32)
    mid = rem.astype(BF16)
    lo = (rem - mid.astype(F32)).astype(BF16)
    imp = _dot(hi, cmap) + _dot(mid, cmap) + _dot(lo, cmap)
    blk = lax.broadcasted_iota(jnp.int32, (tq, LANES), 1)
    qp = pos0 + lax.broadcasted_iota(jnp.int32, (tq, LANES), 0)
    cur = qp // SLC_LEN
    forced = (blk == 0) | (blk == cur) | (blk == cur - 1)
    valid = blk * SLC_LEN <= qp
    v = jnp.where(valid, jnp.where(forced, FORCED_SCORE, imp), MASKED)
    sel = jnp.zeros((tq, LANES), F32)
    blkf = blk.astype(F32)
    for _ in range(k_eff):
        top = jnp.max(v, axis=1, keepdims=True)
        first = jnp.min(jnp.where(v == top, blkf, float(LANES)), axis=1, keepdims=True)
        hit = blkf == first
        sel = jnp.where(hit, 1.0, sel)
        v = jnp.where(hit, -jnp.inf, v)
    sel_ref[...] = jnp.where(valid, sel, 0.0)


def cmp_select(qn, cslab, cmap, gate, *, rows0, n_rows, tq, blocks_per_seq, n_cmp, pos_fixed, k_eff):
    nc = cslab.shape[1]
    nblk = n_rows // tq
    blk0 = rows0 // tq
    if pos_fixed is None:
        pos0_fn = lambda: (pl.program_id(0) % blocks_per_seq) * tq
    else:
        pos0_fn = lambda: pos_fixed
    return pl.pallas_call(
        functools.partial(_cmp_select_body, tq=tq, n_cmp=n_cmp, pos0_fn=pos0_fn, k_eff=k_eff),
        grid=(nblk,),
        in_specs=[pl.BlockSpec((1, NSA_HEADS, tq, 128), lambda i: (0, 0, blk0 + i, 0)),
                  pl.BlockSpec((1, nc, 128), lambda i: (i // blocks_per_seq, 0, 0)),
                  pl.BlockSpec((nc, 128), lambda i: (0, 0)),
                  pl.BlockSpec((tq, 128), lambda i: (blk0 + i, 0))],
        out_specs=[pl.BlockSpec((NSA_HEADS, tq, 128), lambda i: (0, i, 0)),
                   pl.BlockSpec((tq, 128), lambda i: (i, 0))],
        out_shape=[jax.ShapeDtypeStruct((NSA_HEADS, n_rows, 128), F32),
                   jax.ShapeDtypeStruct((n_rows, 128), F32)],
        compiler_params=_cparams("parallel"),
        name="cmp_select",
    )(qn, cslab, cmap, gate)


def _sample_flash_body(pt_ref, *refs, mode, pp, t_new, past, kpos0, lam_init):
    refs = list(refs)
    q_ref = refs.pop(0)
    k_refs = [refs.pop(0) for _ in range(pp)]
    v_refs = [refs.pop(0) for _ in range(pp)] if mode == "diff" else None
    kn_ref = refs.pop(0)
    vn_ref = refs.pop(0) if mode == "diff" else None
    if mode == "slc":
        sel_ref = refs.pop(0)
        e_ref = refs.pop(0)
    if mode in ("slc", "win"):
        gate_ref = refs.pop(0)
    if mode == "mla":
        wuv_ref = refs.pop(0)
    if mode == "diff":
        lam_ref = refs.pop(0)
        sub_ref = refs.pop(0)
    o_ref, qb_ref, m_ref, l_ref, a_ref, acc_ref, p_ref, kpad_ref, vpad_ref = refs
    n_groups = q_ref.shape[1]
    tq = t_new
    nr = n_groups * tq
    g = pl.program_id(0)
    st = pl.program_id(2)
    n_st = pl.num_programs(2)
    tk = pp * PAGE_SIZE

    @pl.when(st == 0)
    def _():
        qb_ref[...] = q_ref[0].reshape(nr, q_ref.shape[3]).astype(BF16)
        m_ref[...] = jnp.full_like(m_ref, M_INIT)
        l_ref[...] = jnp.zeros_like(l_ref)
        acc_ref[...] = jnp.zeros_like(acc_ref)

    if mode == "mla":
        slopes = None
    elif mode == "diff":
        s0 = [_alibi(DIFF_HEADS, r) for r in range(DIFF_REP)] * 2
        s1 = [_alibi(DIFF_HEADS, DIFF_REP + r) for r in range(DIFF_REP)] * 2
        slopes = [jnp.where(g == 0, a, b) for a, b in zip(s0, s1)]
    else:
        slopes = [_alibi(NSA_HEADS, h) for h in range(NSA_HEADS)]

    qb = qb_ref[...]
    s_all = jnp.concatenate([_dot_nt(qb, r[0]) for r in k_refs], axis=1)
    q_pos = past + lax.broadcasted_iota(jnp.int32, (tq, tk), 0)
    k_pos = kpos0 + st * tk + lax.broadcasted_iota(jnp.int32, (tq, tk), 1)
    dist = q_pos - k_pos
    mask = dist >= 0
    if mode == "win":
        mask = jnp.logical_and(mask, dist <= WINDOW)
    if mode == "slc":
        picked = _dot(sel_ref[...].astype(BF16), e_ref[...])
        mask = jnp.logical_and(mask, picked > 0.5)
    _softmax_step(s_all, mask, dist.astype(F32), slopes, m_ref, l_ref, a_ref, p_ref, n_groups, tq)
    pv = jnp.zeros((nr, 128), F32)
    for i in range(pp):
        vb = v_refs[i][0] if mode == "diff" else k_refs[i][0][:, 0:128]
        pv = pv + _dot(p_ref[:, i * PAGE_SIZE:(i + 1) * PAGE_SIZE], vb)
    acc_ref[...] = a_ref[...] * acc_ref[...] + pv

    @pl.when(st == n_st - 1)
    def _():
        kpad_ref[...] = jnp.zeros_like(kpad_ref)
        kpad_ref[0:t_new, :] = kn_ref[...].astype(BF16)
        if mode == "diff":
            vpad_ref[...] = jnp.zeros_like(vpad_ref)
            vpad_ref[0:t_new, :] = vn_ref[...].astype(BF16)
        kb = kpad_ref[...]
        s_new = _dot_nt(qb, kb)
        dn = (lax.broadcasted_iota(jnp.int32, (tq, LANES), 0) - lax.broadcasted_iota(jnp.int32, (tq, LANES), 1))
        mn = jnp.logical_and(dn >= 0, lax.broadcasted_iota(jnp.int32, (tq, LANES), 1) < t_new)
        _softmax_step(s_new, mn, dn.astype(F32), slopes, m_ref, l_ref, a_ref, p_ref, n_groups, tq)
        vb = vpad_ref[...] if mode == "diff" else kb[:, 0:128]
        acc = a_ref[...] * acc_ref[...] + _dot(p_ref[:, 0:LANES], vb)
        if mode == "mla":
            _finish_mla(acc, l_ref[...], wuv_ref, o_ref, tq)
        elif mode == "diff":
            _finish_diff(acc, l_ref[...], lam_ref, sub_ref, lam_init, o_ref, tq)
        else:
            _finish_nsa(acc, l_ref[...], gate_ref[...], 1 if mode == "slc" else 2, o_ref, tq)


def sample_flash(mode, n_seq, t_new, row0, past, q, pages, table, knew, kpos0, pp, sel=None, emat=None,
                 gate=None, wuv=None, lam=None, sub=None, lam_init=0.0):
    n_g, n_groups, m, dk = q.shape
    n_pages = table.shape[1]
    assert n_pages % pp == 0 and row0 % t_new == 0
    n_st = n_pages // pp
    blk0 = row0 // t_new
    kw = 256 if mode == "mla" else 128
    kcol = {"mla": 0, "slc": 1, "win": 0, "diff": None}[mode]
    pcol = {"mla": 0, "slc": 0, "win": 0, "diff": None}[mode]

    def page_map(i, col_fn, g, sq, st, pt):
        return (pt[sq * n_pages + st * pp + i], 0, col_fn(g))

    kfn = (lambda g: g) if mode == "diff" else (lambda g: pcol)
    in_specs = [pl.BlockSpec((1, n_groups, t_new, dk), lambda g, sq, st, pt: (g, 0, blk0 + sq, 0))]
    args = [q]
    in_specs += [pl.BlockSpec((1, PAGE_SIZE, kw), functools.partial(page_map, i, kfn)) for i in range(pp)]
    args += [pages] * pp
    if mode == "diff":
        in_specs += [pl.BlockSpec((1, PAGE_SIZE, 128), functools.partial(page_map, i, lambda g: 2 + g))
                     for i in range(pp)]
        args += [pages] * pp
        in_specs += [pl.BlockSpec((t_new, 128), lambda g, sq, st, pt: (blk0 + sq, g)),
                     pl.BlockSpec((t_new, 128), lambda g, sq, st, pt: (blk0 + sq, 2 + g))]
        args += [knew, knew]
    else:
        in_specs.append(pl.BlockSpec((t_new, kw), lambda g, sq, st, pt: (blk0 + sq, kcol)))
        args.append(knew)
    if mode == "slc":
        in_specs += [pl.BlockSpec((t_new, 128), lambda g, sq, st, pt: (sq, 0)),
                     pl.BlockSpec((128, pp * PAGE_SIZE), lambda g, sq, st, pt: (0, st))]
        args += [sel, emat]
    if mode in ("slc", "win"):
        in_specs.append(pl.BlockSpec((t_new, 128), lambda g, sq, st, pt: (blk0 + sq, 0)))
        args.append(gate)
    if mode == "mla":
        in_specs.append(pl.BlockSpec((MLA_HEADS, 128, 512), lambda g, sq, st, pt: (0, 0, 0)))
        args.append(wuv)
    if mode == "diff":
        in_specs += [pl.BlockSpec((4, DIFF_D), lambda g, sq, st, pt: (0, 0)),
                     pl.BlockSpec((1, 128), lambda g, sq, st, pt: (0, 0))]
        args += [lam, sub]
    rows = n_seq * t_new
    if mode == "mla":
        out_spec = pl.BlockSpec((t_new, 512), lambda g, sq, st, pt: (sq, 0))
        out_shape = jax.ShapeDtypeStruct((rows, 512), F32)
    elif mode == "diff":
        out_spec = pl.BlockSpec((t_new, 512), lambda g, sq, st, pt: (sq, g))
        out_shape = jax.ShapeDtypeStruct((rows, ODD_MIX), F32)
    else:
        out_spec = pl.BlockSpec((NSA_HEADS, t_new, 128), lambda g, sq, st, pt: (0, sq, 0))
        out_shape = jax.ShapeDtypeStruct((NSA_HEADS, rows, 128), F32)
    nr = n_groups * t_new
    grid_spec = pltpu.PrefetchScalarGridSpec(
        num_scalar_prefetch=1,
        grid=(n_g, n_seq, n_st),
        in_specs=in_specs,
        out_specs=out_spec,
        scratch_shapes=[pltpu.VMEM((nr, dk), BF16), pltpu.VMEM((nr, 1), F32), pltpu.VMEM((nr, 1), F32),
                        pltpu.VMEM((nr, 1), F32), pltpu.VMEM((nr, 128), F32),
                        pltpu.VMEM((nr, pp * PAGE_SIZE), BF16),
                        pltpu.VMEM((LANES, kw), BF16), pltpu.VMEM((LANES, 128), BF16)],
    )
    return pl.pallas_call(
        functools.partial(_sample_flash_body, mode=mode, pp=pp, t_new=t_new, past=past, kpos0=kpos0,
                          lam_init=lam_init),
        grid_spec=grid_spec,
        out_shape=out_shape,
        compiler_params=_cparams("parallel", "parallel", "arbitrary"),
        name="sample_flash_" + mode,
    )(table.reshape(-1), *args)


def _even_weights(w_in, gate_b, w_uq, w_uk, w_uv, w_out, cmp_w1, cmp_w2, cmp_pe):
    cuts = np.cumsum(EVEN_SPLITS)[:-1].tolist()
    q_nsa, kv_nsa, kv_win, gate, q_lat, kv_lat, k_rope = jnp.split(w_in, cuts, axis=1)
    pad = jnp.zeros((D_MODEL, EVEN_IN_PAD - sum(EVEN_SPLITS)), F32)
    w = jnp.concatenate([q_nsa, kv_nsa, kv_win, q_lat, kv_lat, k_rope, gate, pad], axis=1).astype(BF16)
    gb = jnp.zeros((1, 128), F32).at[0, GATE_LANE0:GATE_LANE0 + 3 * NSA_HEADS].set(gate_b)
    uq = w_uq.reshape(MLA_Q_RANK, MLA_HEADS, MLA_NOPE + MLA_ROPE)
    wuq = jnp.concatenate([uq[:, :, :MLA_NOPE].reshape(MLA_Q_RANK, -1),
                           uq[:, :, MLA_NOPE:].reshape(MLA_Q_RANK, -1)], axis=1).astype(BF16)
    wuk = jnp.zeros((MLA_HEADS, MLA_NOPE, MLA_HEADS, MLA_KV_RANK), F32)
    wuv = jnp.zeros((MLA_HEADS, MLA_KV_RANK, MLA_HEADS, MLA_V), F32)
    for h in range(MLA_HEADS):
        wuk = wuk.at[h, :, h, :].set(w_uk[h].T)
        wuv = wuv.at[h, :, h, :].set(w_uv[h])
    wuk = wuk.reshape(MLA_HEADS * MLA_NOPE, MLA_HEADS * MLA_KV_RANK).astype(BF16)
    wuv = wuv.reshape(MLA_HEADS, MLA_KV_RANK, MLA_HEADS * MLA_V).astype(BF16)
    n_nsa = NSA_HEADS * NSA_DK
    wn = jnp.concatenate([jnp.zeros((NSA_HEADS, NSA_DK, D_MODEL), F32),
                          w_out[:n_nsa].reshape(NSA_HEADS, NSA_DK, D_MODEL)], axis=1).astype(BF16)
    wm = w_out[n_nsa:].astype(BF16)
    w1r = cmp_w1.reshape(2, 2, CMP_STRIDE, NSA_DK, CMP_HIDDEN)
    wc = jnp.zeros((CMP_STRIDE, 2, NSA_DK, 2, 2, CMP_HIDDEN), F32)
    for part in range(2):
        for half in range(2):
            wc = wc.at[:, part, :, part, half, :].set(w1r[part, half])
    wc = wc.reshape(CMP_STRIDE * 2 * NSA_DK, 4 * CMP_HIDDEN).astype(BF16)
    pe = jnp.zeros((2, 8, CMP_LEN * NSA_DK), F32).at[:, 0, :].set(cmp_pe.reshape(2, -1)).astype(BF16)
    w1 = cmp_w1.reshape(2, CMP_LEN * NSA_DK, CMP_HIDDEN).astype(BF16)
    zeros = jnp.zeros((CMP_HIDDEN, NSA_DK), F32)
    w2k = jnp.concatenate([cmp_w2[0], zeros], axis=1).astype(BF16)
    w2v = jnp.concatenate([zeros, cmp_w2[1]], axis=1).astype(BF16)
    return dict(w=w, gb=gb, wuq=wuq, wuk=wuk, wuv=wuv, wn=wn, wm=wm, wc=wc, pe=pe, w1=w1, w2k=w2k, w2v=w2v)


def _rope_tables(pos):
    half = MLA_ROPE // 2
    inv = ROPE_THETA ** (-jnp.arange(half, dtype=F32) / half)
    ang = pos.astype(F32)[:, None] * inv[None, :]
    cos, sin = jnp.cos(ang), jnp.sin(ang)
    cos = jnp.tile(jnp.concatenate([cos, cos], axis=1), (1, 4))
    sin = jnp.tile(jnp.concatenate([-sin, sin], axis=1), (1, 4))
    return cos, sin


def _cmap(n_rows, n_cmp):
    cs = np.arange(n_rows)[:, None] * CMP_STRIDE
    bs = np.arange(LANES)[None, :] * SLC_LEN
    m = (cs < bs + SLC_LEN) & (cs + CMP_LEN > bs) & (np.arange(n_rows)[:, None] < n_cmp)
    return jnp.asarray(m.astype(np.float32)).astype(BF16)


def _emat(n_keys):
    m = (np.arange(n_keys)[None, :] // SLC_LEN) == np.arange(LANES)[:, None]
    return jnp.asarray(m.astype(np.float32)).astype(BF16)


def kernel(x_prompt, x_sample, cache_nsa, cache_nsa_win, cache_mla, cache_diff, page_table, ln_g, ln_b, ffn_w_in, ffn_w_out, even_w_in, nsa_gate_b, nsa_cmp_pe, nsa_cmp_w1, nsa_cmp_w2, mla_q_norm_g, mla_w_uq, mla_kv_norm_g, mla_w_uk, mla_w_uv, even_w_out, odd_w_in, diff_lambda, diff_subln_g, odd_w_out):
    batch, seq, _ = x_prompt.shape
    n_seq, t_new, _ = x_sample.shape
    n_pages = page_table.shape[1]
    past = n_pages * PAGE_SIZE
    n_pool = cache_nsa.shape[1]
    rows_p = batch * seq
    rows_s = n_seq * t_new
    n_win = cache_nsa_win.shape[2]
    assert seq % 512 == 0 and past % 512 == 0 and n_win % PAGE_SIZE == 0 and t_new == 8

    x = jnp.concatenate([x_prompt.reshape(rows_p, D_MODEL), x_sample.reshape(rows_s, D_MODEL)], axis=0)
    pos = jnp.concatenate([jnp.tile(jnp.arange(seq, dtype=jnp.int32), batch),
                           jnp.tile(past + jnp.arange(t_new, dtype=jnp.int32), n_seq)])
    cos, sin = _rope_tables(pos)
    w_in_bf = ffn_w_in.astype(BF16)
    w_out_bf = ffn_w_out.astype(BF16)
    table = page_table.astype(jnp.int32)
    ident_p = jnp.arange(batch * (seq // PAGE_SIZE), dtype=jnp.int32).reshape(batch, seq // PAGE_SIZE)
    ident_w = jnp.arange(n_seq * (n_win // PAGE_SIZE), dtype=jnp.int32).reshape(n_seq, n_win // PAGE_SIZE)
    emat_p = _emat(seq)
    emat_s = _emat(past)
    cmap_p = _cmap(seq // CMP_STRIDE, seq // CMP_STRIDE - 1)
    cmap_s = _cmap(past // CMP_STRIDE, past // CMP_STRIDE - 1)
    pp = 8 if n_pages % 8 == 0 else n_pages

    outs = {k: [] for k in ("nsa", "win", "mla", "diff")}
    for li in range(DEPTH):
        j = li // 2
        x = ffn_ln(x, w_in_bf[li, 0], w_out_bf[li, 0], ln_g[li, 0], ln_b[li, 0])
        if li % 2 == 0:
            ew = _even_weights(even_w_in[j], nsa_gate_b[j], mla_w_uq[j], mla_w_uk[j], mla_w_uv[j], even_w_out[j],
                               nsa_cmp_w1[j], nsa_cmp_w2[j], nsa_cmp_pe[j])
            kvn, kvw, gate, qn, rows, mk, qm = even_proj(
                x, ew["w"], cos, sin, ew["gb"], mla_q_norm_g[j].reshape(1, -1), mla_kv_norm_g[j].reshape(1, -1),
                ew["wuq"], ew["wuk"])
            qn4 = qn[None]
            qm4 = qm[None]
            chunks_p = kvn[:rows_p, :128].reshape(rows_p // CMP_STRIDE, CMP_STRIDE * 128).astype(BF16)
            f_p = matmul(chunks_p, ew["wc"]).reshape(-1, 8, 512)
            c_p = cmp_combine(f_p, ident_p, ew["pe"], ew["w1"], ew["w2k"], ew["w2v"])
            cn = cache_nsa[j].reshape(n_pool, PAGE_SIZE, 4, NSA_DK)
            chunks_s = cn[:, :, 0:2, :].reshape(n_pool * 8, CMP_STRIDE * 128).astype(BF16)
            f_s = matmul(chunks_s, ew["wc"]).reshape(n_pool, 8, 512)
            c_s = cmp_combine(f_s, table, ew["pe"], ew["w1"], ew["w2k"], ew["w2v"])
            slc_pages = cn[:, :, 2:4, :].reshape(n_pool, PAGE_SIZE, 128).astype(BF16)
            win_pages = cache_nsa_win[j].reshape(n_seq * (n_win // PAGE_SIZE), PAGE_SIZE, 128).astype(BF16)
            cm = cache_mla[j]
            kr = cm[..., MLA_KV_RANK:]
            mla_pages = jnp.concatenate([cm[..., :MLA_KV_RANK], kr, kr, kr, kr], axis=-1).astype(BF16)

            oc_p, sel_p = cmp_select(qn4, c_p, cmap_p, gate, rows0=0, n_rows=rows_p, tq=128,
                                     blocks_per_seq=seq // 128, n_cmp=seq // CMP_STRIDE - 1, pos_fixed=None,
                                     k_eff=N_SELECT)
            oc_s, sel_s = cmp_select(qn4, c_s, cmap_s, gate, rows0=rows_p, n_rows=rows_s, tq=t_new,
                                     blocks_per_seq=1, n_cmp=past // CMP_STRIDE - 1, pos_fixed=past,
                                     k_eff=N_SELECT - 1)
            os_p = prompt_flash("slc", batch, seq, qn4, kvn, sel=sel_p, emat=emat_p, gate=gate)
            ow_p = prompt_flash("win", batch, seq, qn4, kvw, gate=gate)
            om_p = prompt_flash("mla", batch, seq, qm4, mk, wuv=ew["wuv"])
            os_s = sample_flash("slc", n_seq, t_new, rows_p, past, qn4, slc_pages, table, kvn, 0, pp,
                                sel=sel_s, emat=emat_s, gate=gate)
            ow_s = sample_flash("win", n_seq, t_new, rows_p, past, qn4, win_pages, ident_w, kvw, past - n_win,
                                n_win // PAGE_SIZE, gate=gate)
            om_s = sample_flash("mla", n_seq, t_new, rows_p, past, qm4, mla_pages, table, mk, 0, pp,
                                wuv=ew["wuv"])
            cat1 = lambda a, b: jnp.concatenate([a, b], axis=1)
            x = even_out(x, cat1(oc_p, oc_s), cat1(os_p, os_s), cat1(ow_p, ow_s),
                         jnp.concatenate([om_p, om_s], axis=0), ew["wn"], ew["wm"], ln_g[li, 1], ln_b[li, 1])
            outs["nsa"].append(kvn)
            outs["win"].append(kvw)
            outs["mla"].append(rows)
        else:
            lam_init = 0.8 - 0.6 * math.exp(-0.3 * li)
            kv, qd = odd_proj(x, odd_w_in[j].astype(BF16))
            sub = diff_subln_g[j].reshape(1, -1)
            diff_pages = cache_diff[j].reshape(n_pool, PAGE_SIZE, 512).astype(BF16)
            od_p = prompt_flash("diff", batch, seq, qd, kv, lam=diff_lambda[j], sub=sub, lam_init=lam_init)
            od_s = sample_flash("diff", n_seq, t_new, rows_p, past, qd, diff_pages, table, kv, 0, pp,
                                lam=diff_lambda[j], sub=sub, lam_init=lam_init)
            x = odd_out(x, jnp.concatenate([od_p, od_s], axis=0), odd_w_out[j].astype(BF16),
                        ln_g[li, 1], ln_b[li, 1])
            outs["diff"].append(kv)
        x = ffn_ln(x, w_in_bf[li, 1], w_out_bf[li, 1], ln_g[li, 2], ln_b[li, 2])

    def split(a, shape_p, shape_s):
        return a[:rows_p].reshape(shape_p), a[rows_p:].reshape(shape_s)

    y_p, y_s = split(x, (batch, seq, D_MODEL), (n_seq, t_new, D_MODEL))
    nsa = [split(a, (batch, seq, 4, NSA_DK), (n_seq, t_new, 4, NSA_DK)) for a in outs["nsa"]]
    win = [split(a, (batch, seq, 2, NSA_DK), (n_seq, t_new, 2, NSA_DK)) for a in outs["win"]]
    mla = [split(a, (batch, seq, MLA_KV_RANK + MLA_ROPE), (n_seq, t_new, MLA_KV_RANK + MLA_ROPE))
           for a in outs["mla"]]
    dif = [split(a, (batch, seq, 2, DIFF_KV_HEADS, 2 * DIFF_D), (n_seq, t_new, 2, DIFF_KV_HEADS, 2 * DIFF_D))
           for a in outs["diff"]]
    n_keep = min(WINDOW, seq)
    win_p = jnp.stack([w[0][:, seq - n_keep:] for w in win], 0)
    win_s = jnp.stack([jnp.concatenate([cache_nsa_win[i][:, t_new:], w[1]], axis=1) for i, w in enumerate(win)], 0)
    return (y_p, y_s,
            jnp.stack([a[0] for a in nsa], 0), jnp.stack([a[1] for a in nsa], 0),
            win_p, win_s,
            jnp.stack([a[0] for a in mla], 0), jnp.stack([a[1] for a in mla], 0),
            jnp.stack([a[0] for a in dif], 0), jnp.stack([a[1] for a in dif], 0))
```

```python
import functools
import math

import numpy as np
import jax
import jax.numpy as jnp
from jax import lax
from jax.experimental import pallas as pl
from jax.experimental.pallas import tpu as pltpu

D_MODEL = 1024
DEPTH = 4
PAGE_SIZE = 128
ALPHA = (2.0 * DEPTH) ** 0.25
LN_EPS = 1e-5
RMS_EPS = 1e-6
FFN_HIDDEN = ((8 * D_MODEL // 3 + 127) // 128) * 128
NSA_HEADS = 8
NSA_DK = 64
CMP_LEN = 32
CMP_STRIDE = 16
CMP_HIDDEN = 2 * NSA_DK
SLC_LEN = 64
N_SELECT = 16
WINDOW = 512
FORCED_SCORE = 1e9
MLA_HEADS = 8
MLA_Q_RANK = 256
MLA_KV_RANK = 128
MLA_NOPE = 64
MLA_ROPE = 32
MLA_V = 64
ROPE_THETA = 10000.0
DIFF_HEADS = 8
DIFF_KV_HEADS = 2
DIFF_D = 64
DIFF_REP = DIFF_HEADS // DIFF_KV_HEADS
EVEN_SPLITS = (NSA_HEADS * NSA_DK, 4 * NSA_DK, 2 * NSA_DK, 3 * NSA_HEADS, MLA_Q_RANK, MLA_KV_RANK, MLA_ROPE)
ODD_MIX = DIFF_HEADS * 2 * DIFF_D

LANES = 128
MASKED = -1e30
M_INIT = -1e29
VMEM_LIMIT_BYTES = 56 * 2 ** 20
EVEN_IN_PAD = 1408
GATE_LANE0 = 32
BF16 = jnp.bfloat16
F32 = jnp.float32


def _cparams(*sem):
    return pltpu.CompilerParams(dimension_semantics=sem, vmem_limit_bytes=VMEM_LIMIT_BYTES)


def _pick_tile(n, cap, mult=8):
    t = min(cap, n)
    while t > mult and (n % t or t % mult):
        t -= mult
    assert n % t == 0, (n, cap, mult)
    return t


def _layer_norm(y, g, b):
    mu = jnp.mean(y, axis=-1, keepdims=True)
    d = y - mu
    var = jnp.mean(d * d, axis=-1, keepdims=True)
    return d * lax.rsqrt(var + LN_EPS) * g + b


def _rms_norm(x, g):
    return x * lax.rsqrt(jnp.mean(x * x, axis=-1, keepdims=True) + RMS_EPS) * g


def _dot(a, b):
    return jnp.dot(a, b, preferred_element_type=F32)


def _dot_nt(a, b):
    return lax.dot_general(a, b, (((1,), (1,)), ((), ())), preferred_element_type=F32)


def _ffn_ln_body(x_ref, wg_ref, wu_ref, wo_ref, g_ref, b_ref, o_ref, acc_ref, xb_ref, *, nh):
    h = pl.program_id(1)

    @pl.when(h == 0)
    def _():
        acc_ref[...] = jnp.zeros_like(acc_ref)
        xb_ref[...] = x_ref[...].astype(BF16)

    xb = xb_ref[...]
    gate = _dot(xb, wg_ref[...])
    up = _dot(xb, wu_ref[...])
    act = (gate * jax.nn.sigmoid(gate) * up).astype(BF16)
    acc_ref[...] += _dot(act, wo_ref[...])

    @pl.when(h == nh - 1)
    def _():
        y = ALPHA * x_ref[...] + 0.5 * acc_ref[...]
        o_ref[...] = _layer_norm(y, g_ref[...], b_ref[...])


def ffn_ln(x, w_in, w_out, g, b):
    m = x.shape[0]
    tm = _pick_tile(m, 1024)
    th = 256
    nh = FFN_HIDDEN // th
    return pl.pallas_call(
        functools.partial(_ffn_ln_body, nh=nh),
        grid=(m // tm, nh),
        in_specs=[
            pl.BlockSpec((tm, D_MODEL), lambda i, h: (i, 0)),
            pl.BlockSpec((D_MODEL, th), lambda i, h: (0, h)),
            pl.BlockSpec((D_MODEL, th), lambda i, h: (0, h + nh)),
            pl.BlockSpec((th, D_MODEL), lambda i, h: (h, 0)),
            pl.BlockSpec((1, D_MODEL), lambda i, h: (0, 0)),
            pl.BlockSpec((1, D_MODEL), lambda i, h: (0, 0)),
        ],
        out_specs=pl.BlockSpec((tm, D_MODEL), lambda i, h: (i, 0)),
        out_shape=jax.ShapeDtypeStruct((m, D_MODEL), F32),
        scratch_shapes=[pltpu.VMEM((tm, D_MODEL), F32), pltpu.VMEM((tm, D_MODEL), BF16)],
        compiler_params=_cparams("parallel", "arbitrary"),
        name="ffn_ln",
    )(x, w_in, w_in, w_out, g.reshape(1, -1), b.reshape(1, -1))


def _mm_body(x_ref, w_ref, o_ref):
    o_ref[...] = _dot(x_ref[...], w_ref[...])


def matmul(x, w):
    m, k = x.shape
    n = w.shape[1]
    tm = _pick_tile(m, 512, 16)
    return pl.pallas_call(
        _mm_body,
        grid=(m // tm,),
        in_specs=[pl.BlockSpec((tm, k), lambda i: (i, 0)), pl.BlockSpec((k, n), lambda i: (0, 0))],
        out_specs=pl.BlockSpec((tm, n), lambda i: (i, 0)),
        out_shape=jax.ShapeDtypeStruct((m, n), F32),
        compiler_params=_cparams("parallel"),
        name="cmp_matmul",
    )(x, w)


def _even_proj_body(x_ref, w_ref, cos_ref, sin_ref, gb_ref, gq_ref, gkv_ref, wuq_ref, wuk_ref,
                    kvn_ref, kvw_ref, gate_ref, qn_ref, rows_ref, mk_ref, qm_ref):
    tm = x_ref.shape[0]
    z = _dot(x_ref[...].astype(BF16), w_ref[...])
    kvn_ref[...] = z[:, 512:768]
    kvw_ref[...] = z[:, 768:896]
    lane = lax.broadcasted_iota(jnp.int32, (tm, LANES), 1)
    low_half = lane < 64
    nsa_scale = NSA_DK ** -0.5
    for h in range(NSA_HEADS):
        grp = z[:, 128 * (h // 2):128 * (h // 2) + 128]
        if h % 2:
            grp = pltpu.roll(grp, 64, 1)
        qn_ref[h] = jnp.where(low_half, grp * nsa_scale, 0.0)
    slab = z[:, 1280:1408]
    gate_ref[...] = jax.nn.sigmoid(slab + gb_ref[...])
    cos = cos_ref[...]
    sin = sin_ref[...]
    first_half = (lane & 31) < 16

    def rope(v):
        swapped = jnp.where(first_half, pltpu.roll(v, LANES - 16, 1), pltpu.roll(v, 16, 1))
        return v * cos + swapped * sin

    kr = rope(slab)
    c = _rms_norm(z[:, 1152:1280], gkv_ref[...])
    rows_ref[:, 0:128] = c
    rows_ref[:, 128:160] = kr[:, 0:32]
    mk_ref[:, 0:128] = c
    mk_ref[:, 128:256] = jnp.where(lane < 32, kr, 0.0)
    ql = _rms_norm(z[:, 896:1152], gq_ref[...])
    qm = _dot(ql.astype(BF16), wuq_ref[...])
    qabs = _dot(qm[:, 0:512].astype(BF16), wuk_ref[...])
    r0 = rope(qm[:, 512:640])
    r1 = rope(qm[:, 640:768])
    mla_scale = (MLA_NOPE + MLA_ROPE) ** -0.5
    for h in range(MLA_HEADS):
        qm_ref[h, :, 0:128] = qabs[:, 128 * h:128 * h + 128] * mla_scale
        rr = r0 if h < 4 else r1
        if h % 4:
            rr = pltpu.roll(rr, LANES - 32 * (h % 4), 1)
        qm_ref[h, :, 128:256] = jnp.where(lane < 32, rr * mla_scale, 0.0)


def even_proj(x, w, cos, sin, gb, gq, gkv, wuq, wuk):
    m = x.shape[0]
    tm = _pick_tile(m, 512)
    row = lambda i: (i, 0)
    full2 = lambda i: (0, 0)
    out_shapes = [
        jax.ShapeDtypeStruct((m, 256), F32),
        jax.ShapeDtypeStruct((m, 128), F32),
        jax.ShapeDtypeStruct((m, 128), F32),
        jax.ShapeDtypeStruct((NSA_HEADS, m, 128), F32),
        jax.ShapeDtypeStruct((m, 160), F32),
        jax.ShapeDtypeStruct((m, 256), F32),
        jax.ShapeDtypeStruct((MLA_HEADS, m, 256), F32),
    ]
    out_specs = [
        pl.BlockSpec((tm, 256), row), pl.BlockSpec((tm, 128), row), pl.BlockSpec((tm, 128), row),
        pl.BlockSpec((NSA_HEADS, tm, 128), lambda i: (0, i, 0)),
        pl.BlockSpec((tm, 160), row), pl.BlockSpec((tm, 256), row),
        pl.BlockSpec((MLA_HEADS, tm, 256), lambda i: (0, i, 0)),
    ]
    return pl.pallas_call(
        _even_proj_body,
        grid=(m // tm,),
        in_specs=[
            pl.BlockSpec((tm, D_MODEL), row),
            pl.BlockSpec((D_MODEL, EVEN_IN_PAD), full2),
            pl.BlockSpec((tm, 128), row), pl.BlockSpec((tm, 128), row),
            pl.BlockSpec((1, 128), full2), pl.BlockSpec((1, 256), full2), pl.BlockSpec((1, 128), full2),
            pl.BlockSpec((MLA_Q_RANK, 768), full2),
            pl.BlockSpec((512, 1024), full2),
        ],
        out_specs=out_specs,
        out_shape=out_shapes,
        compiler_params=_cparams("parallel"),
        name="even_proj",
    )(x, w, cos, sin, gb, gq, gkv, wuq, wuk)


def _odd_proj_body(x_ref, w_ref, kv_ref, q_ref):
    tm = x_ref.shape[0]
    z = _dot(x_ref[...].astype(BF16), w_ref[...])
    kv_ref[...] = z[:, ODD_MIX:]
    lane = lax.broadcasted_iota(jnp.int32, (tm, LANES), 1)
    scale = DIFF_D ** -0.5
    for g in range(DIFF_KV_HEADS):
        for mm in range(2):
            keep = (lane < 64) if mm == 0 else (lane >= 64)
            for r in range(DIFF_REP):
                hd = g * DIFF_REP + r
                q_ref[g, mm * DIFF_REP + r] = jnp.where(keep, z[:, 128 * hd:128 * hd + 128] * scale, 0.0)


def odd_proj(x, w):
    m = x.shape[0]
    tm = _pick_tile(m, 512)
    return pl.pallas_call(
        _odd_proj_body,
        grid=(m // tm,),
        in_specs=[pl.BlockSpec((tm, D_MODEL), lambda i: (i, 0)),
                  pl.BlockSpec((D_MODEL, w.shape[1]), lambda i: (0, 0))],
        out_specs=[pl.BlockSpec((tm, 512), lambda i: (i, 0)),
                   pl.BlockSpec((DIFF_KV_HEADS, 2 * DIFF_REP, tm, 128), lambda i: (0, 0, i, 0))],
        out_shape=[jax.ShapeDtypeStruct((m, 512), F32),
                   jax.ShapeDtypeStruct((DIFF_KV_HEADS, 2 * DIFF_REP, m, 128), F32)],
        compiler_params=_cparams("parallel"),
        name="odd_proj",
    )(x, w)


def _even_out_body(x_ref, oc_ref, os_ref, ow_ref, om_ref, wn_ref, wm_ref, g_ref, b_ref, o_ref):
    y = _dot(om_ref[...].astype(BF16), wm_ref[...])
    for h in range(NSA_HEADS):
        y = y + _dot((oc_ref[h] + os_ref[h] + ow_ref[h]).astype(BF16), wn_ref[h])
    o_ref[...] = _layer_norm(ALPHA * x_ref[...] + y, g_ref[...], b_ref[...])


def even_out(x, oc, osl, ow, om, wn, wm, g, b):
    m = x.shape[0]
    tm = _pick_tile(m, 512)
    row = lambda i: (i, 0)
    hspec = pl.BlockSpec((NSA_HEADS, tm, 128), lambda i: (0, i, 0))
    return pl.pallas_call(
        _even_out_body,
        grid=(m // tm,),
        in_specs=[pl.BlockSpec((tm, D_MODEL), row), hspec, hspec, hspec,
                  pl.BlockSpec((tm, 512), row),
                  pl.BlockSpec((NSA_HEADS, 128, D_MODEL), lambda i: (0, 0, 0)),
                  pl.BlockSpec((512, D_MODEL), lambda i: (0, 0)),
                  pl.BlockSpec((1, D_MODEL), lambda i: (0, 0)),
                  pl.BlockSpec((1, D_MODEL), lambda i: (0, 0))],
        out_specs=pl.BlockSpec((tm, D_MODEL), row),
        out_shape=jax.ShapeDtypeStruct((m, D_MODEL), F32),
        compiler_params=_cparams("parallel"),
        name="even_out",
    )(x, oc, osl, ow, om, wn, wm, g.reshape(1, -1), b.reshape(1, -1))


def _odd_out_body(x_ref, y_ref, w_ref, g_ref, b_ref, o_ref):
    y = _dot(y_ref[...].astype(BF16), w_ref[...])
    o_ref[...] = _layer_norm(ALPHA * x_ref[...] + y, g_ref[...], b_ref[...])


def odd_out(x, y, w, g, b):
    m = x.shape[0]
    tm = _pick_tile(m, 512)
    row = lambda i: (i, 0)
    return pl.pallas_call(
        _odd_out_body,
        grid=(m // tm,),
        in_specs=[pl.BlockSpec((tm, D_MODEL), row), pl.BlockSpec((tm, ODD_MIX), row),
                  pl.BlockSpec((ODD_MIX, D_MODEL), lambda i: (0, 0)),
                  pl.BlockSpec((1, D_MODEL), lambda i: (0, 0)),
                  pl.BlockSpec((1, D_MODEL), lambda i: (0, 0))],
        out_specs=pl.BlockSpec((tm, D_MODEL), row),
        out_shape=jax.ShapeDtypeStruct((m, D_MODEL), F32),
        compiler_params=_cparams("parallel"),
        name="odd_out",
    )(x, y, w, g.reshape(1, -1), b.reshape(1, -1))


def _cmp_combine_body(pt_ref, *refs, n_pages):
    f_refs = refs[:n_pages]
    pe_ref, w1_ref, w2k_ref, w2v_ref, o_ref = refs[n_pages:]
    f = jnp.concatenate([r[0] for r in f_refs], axis=0)
    n = f.shape[0]
    slab = jnp.zeros((n, 128), F32)
    for part, w2_ref in ((0, w2k_ref), (1, w2v_ref)):
        first = f[:, 256 * part:256 * part + 128]
        second = f[:, 256 * part + 128:256 * part + 256]
        pe_term = _dot(pe_ref[part], w1_ref[part])[0:1]
        hid = first + pltpu.roll(second, n - 1, 0) + pe_term
        slab = slab + _dot(jax.nn.gelu(hid).astype(BF16), w2_ref[...])
    o_ref[0] = slab.astype(BF16)


def cmp_combine(f_pages, table, pe, w1, w2k, w2v):
    n_seq, n_pages = table.shape
    idx = lambda i, s, pt: (pt[s * n_pages + i], 0, 0)
    full3 = lambda s, pt: (0, 0, 0)
    grid_spec = pltpu.PrefetchScalarGridSpec(
        num_scalar_prefetch=1,
        grid=(n_seq,),
        in_specs=[pl.BlockSpec((1, 8, 512), functools.partial(idx, i)) for i in range(n_pages)] + [
            pl.BlockSpec((2, 8, 2048), full3), pl.BlockSpec((2, 2048, 128), full3),
            pl.BlockSpec((128, 128), lambda s, pt: (0, 0)), pl.BlockSpec((128, 128), lambda s, pt: (0, 0))],
        out_specs=pl.BlockSpec((1, 8 * n_pages, 128), lambda s, pt: (s, 0, 0)),
    )
    return pl.pallas_call(
        functools.partial(_cmp_combine_body, n_pages=n_pages),
        grid_spec=grid_spec,
        out_shape=jax.ShapeDtypeStruct((n_seq, 8 * n_pages, 128), BF16),
        compiler_params=_cparams("arbitrary"),
        name="cmp_combine",
    )(table.reshape(-1), *([f_pages] * n_pages), pe, w1, w2k, w2v)


def _softmax_step(s_all, mask, distf, slopes, m_ref, l_ref, n_groups, tq, row0=0):
    nr = n_groups * tq
    rows = slice(row0, row0 + nr)
    m_prev = m_ref[rows]
    l_prev = l_ref[rows]
    m_out, l_out, a_out, p_out = [], [], [], []
    for r in range(n_groups):
        sub = slice(r * tq, (r + 1) * tq)
        s = s_all[sub]
        if slopes is not None:
            s = s - slopes[r] * distf
        s = jnp.where(mask, s, MASKED)
        m_new = jnp.maximum(m_prev[sub], jnp.max(s, axis=1, keepdims=True))
        a = jnp.exp(m_prev[sub] - m_new)
        p = jnp.exp(s - m_new)
        l_out.append(a * l_prev[sub] + jnp.sum(p, axis=1, keepdims=True))
        m_out.append(m_new)
        a_out.append(a)
        p_out.append(p.astype(BF16))
    cat = lambda xs: xs[0] if len(xs) == 1 else jnp.concatenate(xs, axis=0)
    m_ref[rows] = cat(m_out)
    l_ref[rows] = cat(l_out)
    return cat(a_out), cat(p_out)


def _inv_or_zero(l):
    return jnp.where(l > 0.0, 1.0 / l, 0.0)


def _finish_mla(acc, l, wuv_ref, o_ref, tq):
    ol = (acc * _inv_or_zero(l)).astype(BF16)
    out = _dot(ol[0:tq], wuv_ref[0])
    for h in range(1, MLA_HEADS):
        out = out + _dot(ol[h * tq:(h + 1) * tq], wuv_ref[h])
    o_ref[...] = out


def _finish_nsa(acc, l, gate, branch, o_ref, tq):
    o = acc * _inv_or_zero(l)
    if o.shape[1] == NSA_DK:
        o = jnp.concatenate([jnp.zeros_like(o), o], axis=1)
    for h in range(NSA_HEADS):
        c = GATE_LANE0 + NSA_HEADS * branch + h
        o_ref[h] = o[h * tq:(h + 1) * tq] * gate[:, c:c + 1]


def _finish_diff(acc, l, lam_ref, sub_ref, lam_init, o_ref, tq, col0=0):
    lp = lam_ref[...]
    lam = (jnp.exp(jnp.sum(lp[0:1] * lp[1:2], axis=1, keepdims=True))
           - jnp.exp(jnp.sum(lp[2:3] * lp[3:4], axis=1, keepdims=True)) + lam_init)
    o = acc * _inv_or_zero(l)
    half = DIFF_REP * tq
    d = o[0:half] - lam * o[half:2 * half]
    d = _rms_norm(d, sub_ref[...]) * (1.0 - lam_init)
    for r in range(DIFF_REP):
        o_ref[:, col0 + 128 * r:col0 + 128 * r + 128] = d[r * tq:(r + 1) * tq]


def _alibi(n_heads, h):
    return 2.0 ** (-8.0 * (h + 1) / n_heads)


def _prompt_flash_body(qi_ref, kj_ref, fl_ref, ll_ref, *refs, mode, tq, tk, lam_init):
    refs = list(refs)
    q_ref = refs.pop(0)
    k_ref = refs.pop(0)
    v_ref = refs.pop(0) if mode == "diff" else None
    if mode == "slc":
        sel_ref = refs.pop(0)
        e_ref = refs.pop(0)
    if mode in ("slc", "win"):
        gate_ref = refs.pop(0)
    if mode == "mla":
        wuv_ref = refs.pop(0)
    if mode == "diff":
        lam_ref = refs.pop(0)
        sub_ref = refs.pop(0)
    o_ref, qb_ref, m_ref, l_ref, acc_ref = refs
    n_groups = q_ref.shape[1]
    g = pl.program_id(1)
    s = pl.program_id(2)

    @pl.when(fl_ref[s] == 1)
    def _():
        qb_ref[...] = q_ref[0].reshape(n_groups * tq, q_ref.shape[3]).astype(BF16)
        m_ref[...] = jnp.full_like(m_ref, M_INIT)
        l_ref[...] = jnp.zeros_like(l_ref)
        acc_ref[...] = jnp.zeros_like(acc_ref)

    kb = k_ref[...].astype(BF16)
    s_all = _dot_nt(qb_ref[...], kb)
    q_pos = qi_ref[s] * tq + lax.broadcasted_iota(jnp.int32, (tq, tk), 0)
    k_pos = kj_ref[s] * tk + lax.broadcasted_iota(jnp.int32, (tq, tk), 1)
    dist = q_pos - k_pos
    mask = dist >= 0
    if mode == "win":
        mask = jnp.logical_and(mask, dist <= WINDOW)
    if mode == "slc":
        picked = _dot(sel_ref[...].astype(BF16), e_ref[...])
        mask = jnp.logical_and(mask, picked > 0.5)
    if mode == "mla":
        slopes = None
    elif mode == "diff":
        s0 = [_alibi(DIFF_HEADS, r) for r in range(DIFF_REP)] * 2
        s1 = [_alibi(DIFF_HEADS, DIFF_REP + r) for r in range(DIFF_REP)] * 2
        slopes = [jnp.where(g == 0, a, b) for a, b in zip(s0, s1)]
    else:
        slopes = [_alibi(NSA_HEADS, h) for h in range(NSA_HEADS)]
    a, p = _softmax_step(s_all, mask, dist.astype(F32), slopes, m_ref, l_ref, n_groups, tq)
    vb = kb[:, 0:128] if mode != "diff" else v_ref[...].astype(BF16)
    acc_ref[...] = a * acc_ref[...] + _dot(p, vb)

    @pl.when(ll_ref[s] == 1)
    def _():
        if mode == "mla":
            _finish_mla(acc_ref[...], l_ref[...], wuv_ref, o_ref, tq)
        elif mode == "diff":
            _finish_diff(acc_ref[...], l_ref[...], lam_ref, sub_ref, lam_init, o_ref, tq)
        else:
            _finish_nsa(acc_ref[...], l_ref[...], gate_ref[...], 1 if mode == "slc" else 2, o_ref, tq)


def _prompt_steps(seq, tq, tk, mode):
    qi, kj, fl, ll = [], [], [], []
    for i in range(seq // tq):
        hi = (i * tq + tq - 1) // tk
        lo = max(0, (i * tq - WINDOW) // tk) if mode == "win" else 0
        for j in range(lo, hi + 1):
            qi.append(i); kj.append(j); fl.append(int(j == lo)); ll.append(int(j == hi))
    return [jnp.asarray(np.asarray(a, np.int32)) for a in (qi, kj, fl, ll)]


def prompt_flash(mode, batch, seq, q, k, v=None, sel=None, emat=None, gate=None, wuv=None,
                 lam=None, sub=None, lam_init=0.0):
    n_g, n_groups, m, dk = q.shape
    tq, tk = (256, 256) if mode == "win" else (128, 2048)
    tq, tk = min(tq, seq), min(tk, seq)
    nq, nk = seq // tq, seq // tk
    steps = _prompt_steps(seq, tq, tk, mode)
    n_steps = int(steps[0].shape[0])
    qrow = lambda b, g, s, qi, kj, fl, ll: (b * nq + qi[s], 0)
    kcol = {"mla": 0, "slc": 1, "win": 0, "diff": None}[mode]
    kw = 256 if mode == "mla" else 128
    if mode == "diff":
        kmap = lambda b, g, s, qi, kj, fl, ll: (b * nk + kj[s], g)
    else:
        kmap = lambda b, g, s, qi, kj, fl, ll: (b * nk + kj[s], kcol)
    in_specs = [pl.BlockSpec((1, n_groups, tq, dk), lambda b, g, s, qi, kj, fl, ll: (g, 0, b * nq + qi[s], 0)),
                pl.BlockSpec((tk, kw), kmap)]
    args = [q, k]
    if mode == "diff":
        in_specs.append(pl.BlockSpec((tk, 128), lambda b, g, s, qi, kj, fl, ll: (b * nk + kj[s], 2 + g)))
        args.append(k)
    if mode == "slc":
        in_specs += [pl.BlockSpec((tq, 128), qrow),
                     pl.BlockSpec((128, tk), lambda b, g, s, qi, kj, fl, ll: (0, kj[s]))]
        args += [sel, emat]
    if mode in ("slc", "win"):
        in_specs.append(pl.BlockSpec((tq, 128), qrow))
        args.append(gate)
    if mode == "mla":
        in_specs.append(pl.BlockSpec((MLA_HEADS, 128, 512), lambda b, g, s, qi, kj, fl, ll: (0, 0, 0)))
        args.append(wuv)
    if mode == "diff":
        in_specs += [pl.BlockSpec((4, DIFF_D), lambda b, g, s, qi, kj, fl, ll: (0, 0)),
                     pl.BlockSpec((1, 128), lambda b, g, s, qi, kj, fl, ll: (0, 0))]
        args += [lam, sub]
    rows = batch * seq
    if mode == "mla":
        out_spec = pl.BlockSpec((tq, 512), qrow)
        out_shape = jax.ShapeDtypeStruct((rows, 512), F32)
    elif mode == "diff":
        out_spec = pl.BlockSpec((tq, 512), lambda b, g, s, qi, kj, fl, ll: (b * nq + qi[s], g))
        out_shape = jax.ShapeDtypeStruct((rows, ODD_MIX), F32)
    else:
        out_spec = pl.BlockSpec((NSA_HEADS, tq, 128), lambda b, g, s, qi, kj, fl, ll: (0, b * nq + qi[s], 0))
        out_shape = jax.ShapeDtypeStruct((NSA_HEADS, rows, 128), F32)
    nr = n_groups * tq
    grid_spec = pltpu.PrefetchScalarGridSpec(
        num_scalar_prefetch=4,
        grid=(batch, n_g, n_steps),
        in_specs=in_specs,
        out_specs=out_spec,
        scratch_shapes=[pltpu.VMEM((nr, dk), BF16), pltpu.VMEM((nr, 1), F32), pltpu.VMEM((nr, 1), F32),
                        pltpu.VMEM((nr, 128), F32)],
    )
    return pl.pallas_call(
        functools.partial(_prompt_flash_body, mode=mode, tq=tq, tk=tk, lam_init=lam_init),
        grid_spec=grid_spec,
        out_shape=out_shape,
        compiler_params=_cparams("parallel", "parallel", "arbitrary"),
        name="prompt_flash_" + mode,
    )(*steps, *args)


def _cmp_select_body(q_ref, c_ref, cmap_ref, gate_ref, oc_ref, sel_ref, *, tq, n_cmp, pos0_fn, k_eff):
    nc = c_ref.shape[1]
    qb = q_ref[0].reshape(NSA_HEADS * tq, 128).astype(BF16)
    cb = c_ref[0]
    s_all = _dot_nt(qb, cb)
    pos0 = pos0_fn()
    col = lax.broadcasted_iota(jnp.int32, (tq, nc), 1)
    q_pos = pos0 + lax.broadcasted_iota(jnp.int32, (tq, nc), 0)
    dist = q_pos - (col * CMP_STRIDE + (CMP_LEN - 1))
    mask = jnp.logical_and(dist >= 0, col < n_cmp)
    distf = dist.astype(F32)
    gate = gate_ref[...]
    psum = jnp.zeros((tq, nc), F32)
    for h in range(NSA_HEADS):
        s = s_all[h * tq:(h + 1) * tq] - _alibi(NSA_HEADS, h) * distf
        s = jnp.where(mask, s, MASKED)
        e = jnp.where(mask, jnp.exp(s - jnp.max(s, axis=1, keepdims=True)), 0.0)
        p = e * _inv_or_zero(jnp.sum(e, axis=1, keepdims=True))
        psum = psum + p
        c = GATE_LANE0 + h
        oc_ref[h] = _dot(p.astype(BF16), cb) * gate[:, c:c + 1]
    cmap = cmap_ref[...]
    hi = psum.astype(BF16)
    rem = psum - hi.astype(F32)
    mid = rem.astype(BF16)
    lo = (rem - mid.astype(F32)).astype(BF16)
    imp = _dot(hi, cmap) + _dot(mid, cmap) + _dot(lo, cmap)
    blk = lax.broadcasted_iota(jnp.int32, (tq, LANES), 1)
    qp = pos0 + lax.broadcasted_iota(jnp.int32, (tq, LANES), 0)
    cur = qp // SLC_LEN
    forced = (blk == 0) | (blk == cur) | (blk == cur - 1)
    valid = blk * SLC_LEN <= qp
    v = jnp.where(valid, jnp.where(forced, FORCED_SCORE, imp), MASKED)
    sel = jnp.zeros((tq, LANES), F32)
    blkf = blk.astype(F32)
    for _ in range(k_eff):
        top = jnp.max(v, axis=1, keepdims=True)
        first = jnp.min(jnp.where(v == top, blkf, float(LANES)), axis=1, keepdims=True)
        hit = blkf == first
        sel = jnp.where(hit, 1.0, sel)
        v = jnp.where(hit, -jnp.inf, v)
    sel_ref[...] = jnp.where(valid, sel, 0.0)


def cmp_select(qn, cslab, cmap, gate, *, rows0, n_rows, tq, blocks_per_seq, n_cmp, pos_fixed, k_eff):
    nc = cslab.shape[1]
    nblk = n_rows // tq
    blk0 = rows0 // tq
    if pos_fixed is None:
        pos0_fn = lambda: (pl.program_id(0) % blocks_per_seq) * tq
    else:
        pos0_fn = lambda: pos_fixed
    return pl.pallas_call(
        functools.partial(_cmp_select_body, tq=tq, n_cmp=n_cmp, pos0_fn=pos0_fn, k_eff=k_eff),
        grid=(nblk,),
        in_specs=[pl.BlockSpec((1, NSA_HEADS, tq, 128), lambda i: (0, 0, blk0 + i, 0)),
                  pl.BlockSpec((1, nc, 128), lambda i: (i // blocks_per_seq, 0, 0)),
                  pl.BlockSpec((nc, 128), lambda i: (0, 0)),
                  pl.BlockSpec((tq, 128), lambda i: (blk0 + i, 0))],
        out_specs=[pl.BlockSpec((NSA_HEADS, tq, 128), lambda i: (0, i, 0)),
                   pl.BlockSpec((tq, 128), lambda i: (i, 0))],
        out_shape=[jax.ShapeDtypeStruct((NSA_HEADS, n_rows, 128), F32),
                   jax.ShapeDtypeStruct((n_rows, 128), F32)],
        compiler_params=_cparams("parallel"),
        name="cmp_select",
    )(qn, cslab, cmap, gate)


def _cmp_paged_body(pt_ref, *refs, pp):
    page_refs = refs[:pp]
    w_ref, o_ref, sk_ref, sv_ref = refs[pp:]
    for i, p in enumerate(page_refs):
        sk_ref[i * PAGE_SIZE:(i + 1) * PAGE_SIZE, :] = p[0, 0].T
        sv_ref[i * PAGE_SIZE:(i + 1) * PAGE_SIZE, :] = p[0, 1].T
    for part, s_ref in ((0, sk_ref), (1, sv_ref)):
        acc = jnp.zeros((8 * pp, 2 * CMP_HIDDEN), F32)
        for r in range(CMP_STRIDE):
            x = s_ref[pl.ds(r, 8 * pp, stride=CMP_STRIDE), :]
            acc = acc + _dot(x.astype(BF16), w_ref[part, r])
        o_ref[:, 256 * part:256 * part + 256] = acc


def cmp_paged(pages, table, w, pp):
    n_seq, n_pages = table.shape
    n_st = n_pages // pp

    def page_map(i, sq, st, pt):
        return (pt[sq * n_pages + st * pp + i], 0, 0, 0)

    grid_spec = pltpu.PrefetchScalarGridSpec(
        num_scalar_prefetch=1,
        grid=(n_seq, n_st),
        in_specs=[pl.BlockSpec((1, 2, NSA_DK, PAGE_SIZE), functools.partial(page_map, i)) for i in range(pp)] + [
            pl.BlockSpec((2, CMP_STRIDE, NSA_DK, 2 * CMP_HIDDEN), lambda sq, st, pt: (0, 0, 0, 0))],
        out_specs=pl.BlockSpec((8 * pp, 512), lambda sq, st, pt: (sq * n_st + st, 0)),
        scratch_shapes=[pltpu.VMEM((pp * PAGE_SIZE, NSA_DK), F32), pltpu.VMEM((pp * PAGE_SIZE, NSA_DK), F32)],
    )
    return pl.pallas_call(
        functools.partial(_cmp_paged_body, pp=pp),
        grid_spec=grid_spec,
        out_shape=jax.ShapeDtypeStruct((n_seq * n_pages * 8, 512), F32),
        compiler_params=_cparams("parallel", "arbitrary"),
        name="cmp_paged",
    )(table.reshape(-1), *([pages] * pp), w)


def _stack_pages(mode, page_refs, g, dq):
    if mode == "mla":
        kt = jnp.concatenate([r[0] for r in page_refs], axis=1)
        kt = jnp.concatenate([kt, jnp.zeros((dq - kt.shape[0], kt.shape[1]), F32)], axis=0).astype(BF16)
        return kt, kt[0:MLA_KV_RANK], True
    if mode in ("slc", "win"):
        return (jnp.concatenate([r[0, 0] for r in page_refs], axis=1).astype(BF16),
                jnp.concatenate([r[0, 1] for r in page_refs], axis=1).astype(BF16), True)
    return (jnp.concatenate([r[0, pl.ds(g, PAGE_SIZE, stride=4), :] for r in page_refs], axis=0).astype(BF16),
            jnp.concatenate([r[0, pl.ds(2 + g, PAGE_SIZE, stride=4), :] for r in page_refs], axis=0).astype(BF16),
            False)


def _sample_flash_body(pt_ref, *refs, mode, pp, ptok, t_new, past, kpos0, lam_init):
    refs = list(refs)
    q_ref = refs.pop(0)
    page_refs = [refs.pop(0) for _ in range(pp)]
    kn_ref = refs.pop(0)
    if mode == "slc":
        sel_ref = refs.pop(0)
        e_ref = refs.pop(0)
    if mode in ("slc", "win"):
        gate_ref = refs.pop(0)
    if mode == "mla":
        wuv_ref = refs.pop(0)
    if mode == "diff":
        lam_ref = refs.pop(0)
        sub_ref = refs.pop(0)
    o_ref, qb_ref, m_ref, l_ref, acc_ref, kpad_ref, vpad_ref = refs
    n_g, n_groups = q_ref.shape[0], q_ref.shape[1]
    tq = t_new
    nr = n_groups * tq
    st = pl.program_id(1)
    n_st = pl.num_programs(1)
    tk = pp * ptok
    dq = qb_ref.shape[1]

    @pl.when(st == 0)
    def _():
        q = q_ref[...].reshape(n_g * nr, q_ref.shape[3])
        qb_ref[...] = q[:, 0:dq].astype(BF16)
        m_ref[...] = jnp.full_like(m_ref, M_INIT)
        l_ref[...] = jnp.zeros_like(l_ref)
        acc_ref[...] = jnp.zeros_like(acc_ref)

    assert tq & (tq - 1) == 0
    head_col = lax.broadcasted_iota(jnp.int32, (nr, 1), 0) >> (tq.bit_length() - 1)

    def slopes_of(g):
        if mode == "mla":
            return None
        col = jnp.zeros((nr, 1), F32)
        for r in range(n_groups):
            if mode == "diff":
                sl = _alibi(DIFF_HEADS, g * DIFF_REP + r % DIFF_REP)
            else:
                sl = _alibi(NSA_HEADS, r)
            col = jnp.where(head_col == r, sl, col)
        return [col]

    def dist_and_mask(width, k_pos0, k_limit):
        row = lax.broadcasted_iota(jnp.int32, (nr, width), 0)
        colk = lax.broadcasted_iota(jnp.int32, (nr, width), 1)
        dist = (past + (row & (tq - 1))) - (k_pos0 + colk)
        mask = dist >= 0
        if k_limit is not None:
            mask = jnp.logical_and(mask, colk < k_limit)
        if mode == "win":
            mask = jnp.logical_and(mask, dist <= WINDOW)
        return dist, mask

    dist, mask = dist_and_mask(tk, kpos0 + st * tk, None)
    if mode == "slc":
        picked = _dot(sel_ref[...].astype(BF16), e_ref[...])
        mask = jnp.logical_and(mask, jnp.concatenate([picked] * n_groups, axis=0) > 0.5)
    distf = dist.astype(F32)
    for g in range(n_g):
        rows = slice(g * nr, (g + 1) * nr)
        qb = qb_ref[rows]
        kb, vb, tokens_last = _stack_pages(mode, page_refs, g, dq)
        s_all = _dot(qb, kb) if tokens_last else _dot_nt(qb, kb)
        a, p = _softmax_step(s_all, mask, distf, slopes_of(g), m_ref, l_ref, 1, nr, row0=g * nr)
        pv = _dot_nt(p, vb) if tokens_last else _dot(p, vb)
        acc_ref[rows] = a * acc_ref[rows] + pv

    @pl.when(st == n_st - 1)
    def _():
        dn, mn = dist_and_mask(LANES, past, t_new)
        dnf = dn.astype(F32)
        kn = kn_ref[...]
        for g in range(n_g):
            rows = slice(g * nr, (g + 1) * nr)
            if mode == "diff":
                k_new, v_new = kn[:, 128 * g:128 * g + 128], kn[:, 256 + 128 * g:384 + 128 * g]
            elif mode == "mla":
                k_new, v_new = kn, kn[:, 0:128]
            else:
                k_new, v_new = kn[:, 0:NSA_DK], kn[:, NSA_DK:2 * NSA_DK]
            kpad_ref[...] = jnp.zeros_like(kpad_ref)
            kpad_ref[0:t_new, :] = k_new.astype(BF16)
            vpad_ref[...] = jnp.zeros_like(vpad_ref)
            vpad_ref[0:t_new, :] = v_new.astype(BF16)
            s_new = _dot_nt(qb_ref[rows], kpad_ref[...])
            a, p = _softmax_step(s_new, mn, dnf, slopes_of(g), m_ref, l_ref, 1, nr, row0=g * nr)
            acc = a * acc_ref[rows] + _dot(p, vpad_ref[...])
            if mode == "mla":
                _finish_mla(acc, l_ref[rows], wuv_ref, o_ref, tq)
            elif mode == "diff":
                _finish_diff(acc, l_ref[rows], lam_ref, sub_ref, lam_init, o_ref, tq, col0=512 * g)
            else:
                _finish_nsa(acc, l_ref[rows], gate_ref[...], 1 if mode == "slc" else 2, o_ref, tq)


def sample_flash(mode, n_seq, t_new, row0, past, q, pages, table, knew, kpos0, pp, sel=None, emat=None,
                 gate=None, wuv=None, lam=None, sub=None, lam_init=0.0):
    n_g, n_groups, m, dk = q.shape
    n_pages = table.shape[1]
    assert n_pages % pp == 0 and row0 % t_new == 0
    n_st = n_pages // pp
    blk0 = row0 // t_new
    kn_w, kn_col = {"mla": (256, 0), "slc": (128, 1), "win": (128, 0), "diff": (512, 0)}[mode]
    dq, dkk, dv = {"mla": (256, 256, 128), "slc": (64, 64, 64), "win": (64, 64, 64), "diff": (128, 128, 128)}[mode]
    if mode in ("slc", "win"):
        page_block = (1, 2) + pages.shape[2:]
        ptok = pages.shape[3]
    else:
        page_block = (1,) + pages.shape[1:]
        ptok = PAGE_SIZE
    part_blk = 1 if mode == "slc" else 0

    def page_map(i, sq, st, pt):
        return (pt[sq * n_pages + st * pp + i], part_blk) + (0,) * (len(page_block) - 2)

    in_specs = [pl.BlockSpec((n_g, n_groups, t_new, dk), lambda sq, st, pt: (0, 0, blk0 + sq, 0))]
    args = [q]
    in_specs += [pl.BlockSpec(page_block, functools.partial(page_map, i)) for i in range(pp)]
    args += [pages] * pp
    in_specs.append(pl.BlockSpec((t_new, kn_w), lambda sq, st, pt: (blk0 + sq, kn_col)))
    args.append(knew)
    if mode == "slc":
        in_specs += [pl.BlockSpec((t_new, 128), lambda sq, st, pt: (sq, 0)),
                     pl.BlockSpec((128, pp * PAGE_SIZE), lambda sq, st, pt: (0, st))]
        args += [sel, emat]
    if mode in ("slc", "win"):
        in_specs.append(pl.BlockSpec((t_new, 128), lambda sq, st, pt: (blk0 + sq, 0)))
        args.append(gate)
    if mode == "mla":
        in_specs.append(pl.BlockSpec((MLA_HEADS, 128, 512), lambda sq, st, pt: (0, 0, 0)))
        args.append(wuv)
    if mode == "diff":
        in_specs += [pl.BlockSpec((4, DIFF_D), lambda sq, st, pt: (0, 0)),
                     pl.BlockSpec((1, 128), lambda sq, st, pt: (0, 0))]
        args += [lam, sub]
    rows = n_seq * t_new
    if mode == "mla":
        out_spec = pl.BlockSpec((t_new, 512), lambda sq, st, pt: (sq, 0))
        out_shape = jax.ShapeDtypeStruct((rows, 512), F32)
    elif mode == "diff":
        out_spec = pl.BlockSpec((t_new, ODD_MIX), lambda sq, st, pt: (sq, 0))
        out_shape = jax.ShapeDtypeStruct((rows, ODD_MIX), F32)
    else:
        out_spec = pl.BlockSpec((NSA_HEADS, t_new, 128), lambda sq, st, pt: (0, sq, 0))
        out_shape = jax.ShapeDtypeStruct((NSA_HEADS, rows, 128), F32)
    nr = n_g * n_groups * t_new
    grid_spec = pltpu.PrefetchScalarGridSpec(
        num_scalar_prefetch=1,
        grid=(n_seq, n_st),
        in_specs=in_specs,
        out_specs=out_spec,
        scratch_shapes=[pltpu.VMEM((nr, dq), BF16), pltpu.VMEM((nr, 1), F32), pltpu.VMEM((nr, 1), F32),
                        pltpu.VMEM((nr, dv), F32),
                        pltpu.VMEM((LANES, dkk), BF16), pltpu.VMEM((LANES, dv), BF16)],
    )
    return pl.pallas_call(
        functools.partial(_sample_flash_body, mode=mode, pp=pp, ptok=ptok, t_new=t_new, past=past, kpos0=kpos0,
                          lam_init=lam_init),
        grid_spec=grid_spec,
        out_shape=out_shape,
        compiler_params=_cparams("parallel", "arbitrary"),
        name="sample_flash_" + mode,
    )(table.reshape(-1), *args)


def _even_weights(w_in, gate_b, w_uq, w_uk, w_uv, w_out, cmp_w1, cmp_w2, cmp_pe):
    cuts = np.cumsum(EVEN_SPLITS)[:-1].tolist()
    q_nsa, kv_nsa, kv_win, gate, q_lat, kv_lat, k_rope = jnp.split(w_in, cuts, axis=1)
    pad = jnp.zeros((D_MODEL, EVEN_IN_PAD - sum(EVEN_SPLITS)), F32)
    w = jnp.concatenate([q_nsa, kv_nsa, kv_win, q_lat, kv_lat, k_rope, gate, pad], axis=1).astype(BF16)
    gb = jnp.zeros((1, 128), F32).at[0, GATE_LANE0:GATE_LANE0 + 3 * NSA_HEADS].set(gate_b)
    uq = w_uq.reshape(MLA_Q_RANK, MLA_HEADS, MLA_NOPE + MLA_ROPE)
    wuq = jnp.concatenate([uq[:, :, :MLA_NOPE].reshape(MLA_Q_RANK, -1),
                           uq[:, :, MLA_NOPE:].reshape(MLA_Q_RANK, -1)], axis=1).astype(BF16)
    wuk = jnp.zeros((MLA_HEADS, MLA_NOPE, MLA_HEADS, MLA_KV_RANK), F32)
    wuv = jnp.zeros((MLA_HEADS, MLA_KV_RANK, MLA_HEADS, MLA_V), F32)
    for h in range(MLA_HEADS):
        wuk = wuk.at[h, :, h, :].set(w_uk[h].T)
        wuv = wuv.at[h, :, h, :].set(w_uv[h])
    wuk = wuk.reshape(MLA_HEADS * MLA_NOPE, MLA_HEADS * MLA_KV_RANK).astype(BF16)
    wuv = wuv.reshape(MLA_HEADS, MLA_KV_RANK, MLA_HEADS * MLA_V).astype(BF16)
    n_nsa = NSA_HEADS * NSA_DK
    wn = jnp.concatenate([jnp.zeros((NSA_HEADS, NSA_DK, D_MODEL), F32),
                          w_out[:n_nsa].reshape(NSA_HEADS, NSA_DK, D_MODEL)], axis=1).astype(BF16)
    wm = w_out[n_nsa:].astype(BF16)
    w1r = cmp_w1.reshape(2, 2, CMP_STRIDE, NSA_DK, CMP_HIDDEN)
    wc = jnp.zeros((CMP_STRIDE, 2, NSA_DK, 2, 2, CMP_HIDDEN), F32)
    for part in range(2):
        for half in range(2):
            wc = wc.at[:, part, :, part, half, :].set(w1r[part, half])
    wc = wc.reshape(CMP_STRIDE * 2 * NSA_DK, 4 * CMP_HIDDEN).astype(BF16)
    wp = jnp.transpose(w1r, (0, 2, 3, 1, 4)).reshape(2, CMP_STRIDE, NSA_DK, 2 * CMP_HIDDEN).astype(BF16)
    pe = jnp.zeros((2, 8, CMP_LEN * NSA_DK), F32).at[:, 0, :].set(cmp_pe.reshape(2, -1)).astype(BF16)
    w1 = cmp_w1.reshape(2, CMP_LEN * NSA_DK, CMP_HIDDEN).astype(BF16)
    zeros = jnp.zeros((CMP_HIDDEN, NSA_DK), F32)
    w2k = jnp.concatenate([cmp_w2[0], zeros], axis=1).astype(BF16)
    w2v = jnp.concatenate([zeros, cmp_w2[1]], axis=1).astype(BF16)
    return dict(w=w, gb=gb, wuq=wuq, wuk=wuk, wuv=wuv, wn=wn, wm=wm, wc=wc, wp=wp, pe=pe, w1=w1, w2k=w2k, w2v=w2v)


def _rope_tables(pos):
    half = MLA_ROPE // 2
    inv = ROPE_THETA ** (-jnp.arange(half, dtype=F32) / half)
    ang = pos.astype(F32)[:, None] * inv[None, :]
    cos, sin = jnp.cos(ang), jnp.sin(ang)
    cos = jnp.tile(jnp.concatenate([cos, cos], axis=1), (1, 4))
    sin = jnp.tile(jnp.concatenate([-sin, sin], axis=1), (1, 4))
    return cos, sin


def _cmap(n_rows, n_cmp):
    cs = np.arange(n_rows)[:, None] * CMP_STRIDE
    bs = np.arange(LANES)[None, :] * SLC_LEN
    m = (cs < bs + SLC_LEN) & (cs + CMP_LEN > bs) & (np.arange(n_rows)[:, None] < n_cmp)
    return jnp.asarray(m.astype(np.float32)).astype(BF16)


def _emat(n_keys):
    m = (np.arange(n_keys)[None, :] // SLC_LEN) == np.arange(LANES)[:, None]
    return jnp.asarray(m.astype(np.float32)).astype(BF16)


def kernel(x_prompt, x_sample, cache_nsa, cache_nsa_win, cache_mla, cache_diff, page_table, ln_g, ln_b, ffn_w_in, ffn_w_out, even_w_in, nsa_gate_b, nsa_cmp_pe, nsa_cmp_w1, nsa_cmp_w2, mla_q_norm_g, mla_w_uq, mla_kv_norm_g, mla_w_uk, mla_w_uv, even_w_out, odd_w_in, diff_lambda, diff_subln_g, odd_w_out):
    batch, seq, _ = x_prompt.shape
    n_seq, t_new, _ = x_sample.shape
    n_pages = page_table.shape[1]
    past = n_pages * PAGE_SIZE
    n_pool = cache_nsa.shape[1]
    rows_p = batch * seq
    rows_s = n_seq * t_new
    n_win = cache_nsa_win.shape[2]
    assert seq % 512 == 0 and past % 512 == 0 and n_win % PAGE_SIZE == 0 and t_new == 8

    x = jnp.concatenate([x_prompt.reshape(rows_p, D_MODEL), x_sample.reshape(rows_s, D_MODEL)], axis=0)
    pos = jnp.concatenate([jnp.tile(jnp.arange(seq, dtype=jnp.int32), batch),
                           jnp.tile(past + jnp.arange(t_new, dtype=jnp.int32), n_seq)])
    cos, sin = _rope_tables(pos)
    w_in_bf = ffn_w_in.astype(BF16)
    w_out_bf = ffn_w_out.astype(BF16)
    table = page_table.astype(jnp.int32)
    ident_p = jnp.arange(batch * (seq // PAGE_SIZE), dtype=jnp.int32).reshape(batch, seq // PAGE_SIZE)
    emat_p = _emat(seq)
    emat_s = _emat(past)
    cmap_p = _cmap(seq // CMP_STRIDE, seq // CMP_STRIDE - 1)
    cmap_s = _cmap(past // CMP_STRIDE, past // CMP_STRIDE - 1)
    pp = 8 if n_pages % 8 == 0 else n_pages
    pp_cmp = 16 if n_pages % 16 == 0 else n_pages
    ident_s = jnp.arange(n_seq * n_pages, dtype=jnp.int32).reshape(n_seq, n_pages)
    nsa_pages = jnp.transpose(cache_nsa, (0, 1, 3, 4, 2)).reshape(-1, 4, NSA_DK, PAGE_SIZE)
    win_pages = jnp.transpose(cache_nsa_win, (0, 1, 3, 4, 2)).reshape(-1, 2, NSA_DK, n_win)
    mla_pages = jnp.transpose(cache_mla, (0, 1, 3, 2)).reshape(-1, MLA_KV_RANK + MLA_ROPE, PAGE_SIZE)
    diff_pages = cache_diff.reshape(-1, 4 * PAGE_SIZE, 2 * DIFF_D)
    ident_w = jnp.arange(n_seq, dtype=jnp.int32).reshape(n_seq, 1)

    outs = {k: [] for k in ("nsa", "win", "mla", "diff")}
    for li in range(DEPTH):
        j = li // 2
        x = ffn_ln(x, w_in_bf[li, 0], w_out_bf[li, 0], ln_g[li, 0], ln_b[li, 0])
        if li % 2 == 0:
            ew = _even_weights(even_w_in[j], nsa_gate_b[j], mla_w_uq[j], mla_w_uk[j], mla_w_uv[j], even_w_out[j],
                               nsa_cmp_w1[j], nsa_cmp_w2[j], nsa_cmp_pe[j])
            kvn, kvw, gate, qn, rows, mk, qm = even_proj(
                x, ew["w"], cos, sin, ew["gb"], mla_q_norm_g[j].reshape(1, -1), mla_kv_norm_g[j].reshape(1, -1),
                ew["wuq"], ew["wuk"])
            qn4 = qn[None]
            qm4 = qm[None]
            chunks_p = kvn[:rows_p, :128].reshape(rows_p // CMP_STRIDE, CMP_STRIDE * 128).astype(BF16)
            f_p = matmul(chunks_p, ew["wc"]).reshape(-1, 8, 512)
            c_p = cmp_combine(f_p, ident_p, ew["pe"], ew["w1"], ew["w2k"], ew["w2v"])
            table_j = table + j * n_pool
            f_s = cmp_paged(nsa_pages, table_j, ew["wp"], pp_cmp).reshape(n_seq * n_pages, 8, 512)
            c_s = cmp_combine(f_s, ident_s, ew["pe"], ew["w1"], ew["w2k"], ew["w2v"])

            oc_p, sel_p = cmp_select(qn4, c_p, cmap_p, gate, rows0=0, n_rows=rows_p, tq=128,
                                     blocks_per_seq=seq // 128, n_cmp=seq // CMP_STRIDE - 1, pos_fixed=None,
                                     k_eff=N_SELECT)
            oc_s, sel_s = cmp_select(qn4, c_s, cmap_s, gate, rows0=rows_p, n_rows=rows_s, tq=t_new,
                                     blocks_per_seq=1, n_cmp=past // CMP_STRIDE - 1, pos_fixed=past,
                                     k_eff=N_SELECT - 1)
            os_p = prompt_flash("slc", batch, seq, qn4, kvn, sel=sel_p, emat=emat_p, gate=gate)
            ow_p = prompt_flash("win", batch, seq, qn4, kvw, gate=gate)
            om_p = prompt_flash("mla", batch, seq, qm4, mk, wuv=ew["wuv"])
            os_s = sample_flash("slc", n_seq, t_new, rows_p, past, qn4, nsa_pages, table_j, kvn, 0, pp,
                                sel=sel_s, emat=emat_s, gate=gate)
            ow_s = sample_flash("win", n_seq, t_new, rows_p, past, qn4, win_pages, ident_w + j * n_seq, kvw,
                                past - n_win, 1, gate=gate)
            om_s = sample_flash("mla", n_seq, t_new, rows_p, past, qm4, mla_pages, table_j, mk, 0, pp,
                                wuv=ew["wuv"])
            cat1 = lambda a, b: jnp.concatenate([a, b], axis=1)
            x = even_out(x, cat1(oc_p, oc_s), cat1(os_p, os_s), cat1(ow_p, ow_s),
                         jnp.concatenate([om_p, om_s], axis=0), ew["wn"], ew["wm"], ln_g[li, 1], ln_b[li, 1])
            outs["nsa"].append(kvn)
            outs["win"].append(kvw)
            outs["mla"].append(rows)
        else:
            lam_init = 0.8 - 0.6 * math.exp(-0.3 * li)
            kv, qd = odd_proj(x, odd_w_in[j].astype(BF16))
            sub = diff_subln_g[j].reshape(1, -1)
            od_p = prompt_flash("diff", batch, seq, qd, kv, lam=diff_lambda[j], sub=sub, lam_init=lam_init)
            od_s = sample_flash("diff", n_seq, t_new, rows_p, past, qd, diff_pages, table + j * n_pool, kv, 0, pp,
                                lam=diff_lambda[j], sub=sub, lam_init=lam_init)
            x = odd_out(x, jnp.concatenate([od_p, od_s], axis=0), odd_w_out[j].astype(BF16),
                        ln_g[li, 1], ln_b[li, 1])
            outs["diff"].append(kv)
        x = ffn_ln(x, w_in_bf[li, 1], w_out_bf[li, 1], ln_g[li, 2], ln_b[li, 2])

    def split(a, shape_p, shape_s):
        return a[:rows_p].reshape(shape_p), a[rows_p:].reshape(shape_s)

    y_p, y_s = split(x, (batch, seq, D_MODEL), (n_seq, t_new, D_MODEL))
    nsa = [split(a, (batch, seq, 4, NSA_DK), (n_seq, t_new, 4, NSA_DK)) for a in outs["nsa"]]
    win = [split(a, (batch, seq, 2, NSA_DK), (n_seq, t_new, 2, NSA_DK)) for a in outs["win"]]
    mla = [split(a, (batch, seq, MLA_KV_RANK + MLA_ROPE), (n_seq, t_new, MLA_KV_RANK + MLA_ROPE))
           for a in outs["mla"]]
    dif = [split(a, (batch, seq, 2, DIFF_KV_HEADS, 2 * DIFF_D), (n_seq, t_new, 2, DIFF_KV_HEADS, 2 * DIFF_D))
           for a in outs["diff"]]
    n_keep = min(WINDOW, seq)
    win_p = jnp.stack([w[0][:, seq - n_keep:] for w in win], 0)
    win_s = jnp.stack([jnp.concatenate([cache_nsa_win[i][:, t_new:], w[1]], axis=1) for i, w in enumerate(win)], 0)
    return (y_p, y_s,
            jnp.stack([a[0] for a in nsa], 0), jnp.stack([a[1] for a in nsa], 0),
            win_p, win_s,
            jnp.stack([a[0] for a in mla], 0), jnp.stack([a[1] for a in mla], 0),
            jnp.stack([a[0] for a in dif], 0), jnp.stack([a[1] for a in dif], 0))
```

```python
import functools
import math

import numpy as np
import jax
import jax.numpy as jnp
from jax import lax
from jax.experimental import pallas as pl
from jax.experimental.pallas import tpu as pltpu

D_MODEL = 1024
DEPTH = 4
PAGE_SIZE = 128
ALPHA = (2.0 * DEPTH) ** 0.25
LN_EPS = 1e-5
RMS_EPS = 1e-6
FFN_HIDDEN = ((8 * D_MODEL // 3 + 127) // 128) * 128
NSA_HEADS = 8
NSA_DK = 64
CMP_LEN = 32
CMP_STRIDE = 16
CMP_HIDDEN = 2 * NSA_DK
SLC_LEN = 64
N_SELECT = 16
WINDOW = 512
FORCED_SCORE = 1e9
MLA_HEADS = 8
MLA_Q_RANK = 256
MLA_KV_RANK = 128
MLA_NOPE = 64
MLA_ROPE = 32
MLA_V = 64
ROPE_THETA = 10000.0
DIFF_HEADS = 8
DIFF_KV_HEADS = 2
DIFF_D = 64
DIFF_REP = DIFF_HEADS // DIFF_KV_HEADS
EVEN_SPLITS = (NSA_HEADS * NSA_DK, 4 * NSA_DK, 2 * NSA_DK, 3 * NSA_HEADS, MLA_Q_RANK, MLA_KV_RANK, MLA_ROPE)
ODD_MIX = DIFF_HEADS * 2 * DIFF_D

LANES = 128
MASKED = -1e30
M_INIT = -1e29
VMEM_LIMIT_BYTES = 56 * 2 ** 20
EVEN_IN_PAD = 1408
GATE_LANE0 = 32
BF16 = jnp.bfloat16
F32 = jnp.float32


def _cparams(*sem):
    return pltpu.CompilerParams(dimension_semantics=sem, vmem_limit_bytes=VMEM_LIMIT_BYTES)


def _pick_tile(n, cap, mult=8):
    t = min(cap, n)
    while t > mult and (n % t or t % mult):
        t -= mult
    assert n % t == 0, (n, cap, mult)
    return t


def _layer_norm(y, g, b):
    mu = jnp.mean(y, axis=-1, keepdims=True)
    d = y - mu
    var = jnp.mean(d * d, axis=-1, keepdims=True)
    return d * lax.rsqrt(var + LN_EPS) * g + b


def _rms_norm(x, g):
    return x * lax.rsqrt(jnp.mean(x * x, axis=-1, keepdims=True) + RMS_EPS) * g


def _dot(a, b):
    return jnp.dot(a, b, preferred_element_type=F32)


def _dot_nt(a, b):
    return lax.dot_general(a, b, (((1,), (1,)), ((), ())), preferred_element_type=F32)


def _ffn_ln_body(x_ref, wg_ref, wu_ref, wo_ref, g_ref, b_ref, o_ref, acc_ref, xb_ref, *, nh):
    h = pl.program_id(1)

    @pl.when(h == 0)
    def _():
        acc_ref[...] = jnp.zeros_like(acc_ref)
        xb_ref[...] = x_ref[...].astype(BF16)

    xb = xb_ref[...]
    gate = _dot(xb, wg_ref[...])
    up = _dot(xb, wu_ref[...])
    act = (gate * jax.nn.sigmoid(gate) * up).astype(BF16)
    acc_ref[...] += _dot(act, wo_ref[...])

    @pl.when(h == nh - 1)
    def _():
        y = ALPHA * x_ref[...] + 0.5 * acc_ref[...]
        o_ref[...] = _layer_norm(y, g_ref[...], b_ref[...])


def ffn_ln(x, w_in, w_out, g, b):
    m = x.shape[0]
    tm = _pick_tile(m, 1024)
    th = 256
    nh = FFN_HIDDEN // th
    return pl.pallas_call(
        functools.partial(_ffn_ln_body, nh=nh),
        grid=(m // tm, nh),
        in_specs=[
            pl.BlockSpec((tm, D_MODEL), lambda i, h: (i, 0)),
            pl.BlockSpec((D_MODEL, th), lambda i, h: (0, h)),
            pl.BlockSpec((D_MODEL, th), lambda i, h: (0, h + nh)),
            pl.BlockSpec((th, D_MODEL), lambda i, h: (h, 0)),
            pl.BlockSpec((1, D_MODEL), lambda i, h: (0, 0)),
            pl.BlockSpec((1, D_MODEL), lambda i, h: (0, 0)),
        ],
        out_specs=pl.BlockSpec((tm, D_MODEL), lambda i, h: (i, 0)),
        out_shape=jax.ShapeDtypeStruct((m, D_MODEL), F32),
        scratch_shapes=[pltpu.VMEM((tm, D_MODEL), F32), pltpu.VMEM((tm, D_MODEL), BF16)],
        compiler_params=_cparams("parallel", "arbitrary"),
        name="ffn_ln",
    )(x, w_in, w_in, w_out, g.reshape(1, -1), b.reshape(1, -1))


def _mm_body(x_ref, w_ref, o_ref):
    o_ref[...] = _dot(x_ref[...], w_ref[...])


def matmul(x, w):
    m, k = x.shape
    n = w.shape[1]
    tm = _pick_tile(m, 512, 16)
    return pl.pallas_call(
        _mm_body,
        grid=(m // tm,),
        in_specs=[pl.BlockSpec((tm, k), lambda i: (i, 0)), pl.BlockSpec((k, n), lambda i: (0, 0))],
        out_specs=pl.BlockSpec((tm, n), lambda i: (i, 0)),
        out_shape=jax.ShapeDtypeStruct((m, n), F32),
        compiler_params=_cparams("parallel"),
        name="cmp_matmul",
    )(x, w)


def _even_proj_body(x_ref, w_ref, cos_ref, sin_ref, gb_ref, gq_ref, gkv_ref, wuq_ref, wuk_ref,
                    kvn_ref, kvw_ref, gate_ref, qn_ref, rows_ref, mk_ref, qm_ref):
    tm = x_ref.shape[0]
    z = _dot(x_ref[...].astype(BF16), w_ref[...])
    kvn_ref[...] = z[:, 512:768]
    kvw_ref[...] = z[:, 768:896]
    lane = lax.broadcasted_iota(jnp.int32, (tm, LANES), 1)
    low_half = lane < 64
    nsa_scale = NSA_DK ** -0.5
    for h in range(NSA_HEADS):
        grp = z[:, 128 * (h // 2):128 * (h // 2) + 128]
        if h % 2:
            grp = pltpu.roll(grp, 64, 1)
        qn_ref[h] = jnp.where(low_half, grp * nsa_scale, 0.0)
    slab = z[:, 1280:1408]
    gate_ref[...] = jax.nn.sigmoid(slab + gb_ref[...])
    cos = cos_ref[...]
    sin = sin_ref[...]
    first_half = (lane & 31) < 16

    def rope(v):
        swapped = jnp.where(first_half, pltpu.roll(v, LANES - 16, 1), pltpu.roll(v, 16, 1))
        return v * cos + swapped * sin

    kr = rope(slab)
    c = _rms_norm(z[:, 1152:1280], gkv_ref[...])
    rows_ref[:, 0:128] = c
    rows_ref[:, 128:160] = kr[:, 0:32]
    mk_ref[:, 0:128] = c
    mk_ref[:, 128:256] = jnp.where(lane < 32, kr, 0.0)
    ql = _rms_norm(z[:, 896:1152], gq_ref[...])
    qm = _dot(ql.astype(BF16), wuq_ref[...])
    qabs = _dot(qm[:, 0:512].astype(BF16), wuk_ref[...])
    r0 = rope(qm[:, 512:640])
    r1 = rope(qm[:, 640:768])
    mla_scale = (MLA_NOPE + MLA_ROPE) ** -0.5
    for h in range(MLA_HEADS):
        qm_ref[h, :, 0:128] = qabs[:, 128 * h:128 * h + 128] * mla_scale
        rr = r0 if h < 4 else r1
        if h % 4:
            rr = pltpu.roll(rr, LANES - 32 * (h % 4), 1)
        qm_ref[h, :, 128:256] = jnp.where(lane < 32, rr * mla_scale, 0.0)


def even_proj(x, w, cos, sin, gb, gq, gkv, wuq, wuk):
    m = x.shape[0]
    tm = _pick_tile(m, 512)
    row = lambda i: (i, 0)
    full2 = lambda i: (0, 0)
    out_shapes = [
        jax.ShapeDtypeStruct((m, 256), F32),
        jax.ShapeDtypeStruct((m, 128), F32),
        jax.ShapeDtypeStruct((m, 128), F32),
        jax.ShapeDtypeStruct((NSA_HEADS, m, 128), F32),
        jax.ShapeDtypeStruct((m, 160), F32),
        jax.ShapeDtypeStruct((m, 256), F32),
        jax.ShapeDtypeStruct((MLA_HEADS, m, 256), F32),
    ]
    out_specs = [
        pl.BlockSpec((tm, 256), row), pl.BlockSpec((tm, 128), row), pl.BlockSpec((tm, 128), row),
        pl.BlockSpec((NSA_HEADS, tm, 128), lambda i: (0, i, 0)),
        pl.BlockSpec((tm, 160), row), pl.BlockSpec((tm, 256), row),
        pl.BlockSpec((MLA_HEADS, tm, 256), lambda i: (0, i, 0)),
    ]
    return pl.pallas_call(
        _even_proj_body,
        grid=(m // tm,),
        in_specs=[
            pl.BlockSpec((tm, D_MODEL), row),
            pl.BlockSpec((D_MODEL, EVEN_IN_PAD), full2),
            pl.BlockSpec((tm, 128), row), pl.BlockSpec((tm, 128), row),
            pl.BlockSpec((1, 128), full2), pl.BlockSpec((1, 256), full2), pl.BlockSpec((1, 128), full2),
            pl.BlockSpec((MLA_Q_RANK, 768), full2),
            pl.BlockSpec((512, 1024), full2),
        ],
        out_specs=out_specs,
        out_shape=out_shapes,
        compiler_params=_cparams("parallel"),
        name="even_proj",
    )(x, w, cos, sin, gb, gq, gkv, wuq, wuk)


def _odd_proj_body(x_ref, w_ref, kv_ref, q_ref):
    tm = x_ref.shape[0]
    z = _dot(x_ref[...].astype(BF16), w_ref[...])
    kv_ref[...] = z[:, ODD_MIX:]
    lane = lax.broadcasted_iota(jnp.int32, (tm, LANES), 1)
    scale = DIFF_D ** -0.5
    for g in range(DIFF_KV_HEADS):
        for mm in range(2):
            keep = (lane < 64) if mm == 0 else (lane >= 64)
            for r in range(DIFF_REP):
                hd = g * DIFF_REP + r
                q_ref[g, mm * DIFF_REP + r] = jnp.where(keep, z[:, 128 * hd:128 * hd + 128] * scale, 0.0)


def odd_proj(x, w):
    m = x.shape[0]
    tm = _pick_tile(m, 512)
    return pl.pallas_call(
        _odd_proj_body,
        grid=(m // tm,),
        in_specs=[pl.BlockSpec((tm, D_MODEL), lambda i: (i, 0)),
                  pl.BlockSpec((D_MODEL, w.shape[1]), lambda i: (0, 0))],
        out_specs=[pl.BlockSpec((tm, 512), lambda i: (i, 0)),
                   pl.BlockSpec((DIFF_KV_HEADS, 2 * DIFF_REP, tm, 128), lambda i: (0, 0, i, 0))],
        out_shape=[jax.ShapeDtypeStruct((m, 512), F32),
                   jax.ShapeDtypeStruct((DIFF_KV_HEADS, 2 * DIFF_REP, m, 128), F32)],
        compiler_params=_cparams("parallel"),
        name="odd_proj",
    )(x, w)


def _even_out_body(x_ref, oc_ref, os_ref, ow_ref, om_ref, wn_ref, wm_ref, g_ref, b_ref, o_ref):
    y = _dot(om_ref[...].astype(BF16), wm_ref[...])
    for h in range(NSA_HEADS):
        y = y + _dot((oc_ref[h] + os_ref[h] + ow_ref[h]).astype(BF16), wn_ref[h])
    o_ref[...] = _layer_norm(ALPHA * x_ref[...] + y, g_ref[...], b_ref[...])


def even_out(x, oc, osl, ow, om, wn, wm, g, b):
    m = x.shape[0]
    tm = _pick_tile(m, 512)
    row = lambda i: (i, 0)
    hspec = pl.BlockSpec((NSA_HEADS, tm, 128), lambda i: (0, i, 0))
    return pl.pallas_call(
        _even_out_body,
        grid=(m // tm,),
        in_specs=[pl.BlockSpec((tm, D_MODEL), row), hspec, hspec, hspec,
                  pl.BlockSpec((tm, 512), row),
                  pl.BlockSpec((NSA_HEADS, 128, D_MODEL), lambda i: (0, 0, 0)),
                  pl.BlockSpec((512, D_MODEL), lambda i: (0, 0)),
                  pl.BlockSpec((1, D_MODEL), lambda i: (0, 0)),
                  pl.BlockSpec((1, D_MODEL), lambda i: (0, 0))],
        out_specs=pl.BlockSpec((tm, D_MODEL), row),
        out_shape=jax.ShapeDtypeStruct((m, D_MODEL), F32),
        compiler_params=_cparams("parallel"),
        name="even_out",
    )(x, oc, osl, ow, om, wn, wm, g.reshape(1, -1), b.reshape(1, -1))


def _odd_out_body(x_ref, y_ref, w_ref, g_ref, b_ref, o_ref):
    y = _dot(y_ref[...].astype(BF16), w_ref[...])
    o_ref[...] = _layer_norm(ALPHA * x_ref[...] + y, g_ref[...], b_ref[...])


def odd_out(x, y, w, g, b):
    m = x.shape[0]
    tm = _pick_tile(m, 512)
    row = lambda i: (i, 0)
    return pl.pallas_call(
        _odd_out_body,
        grid=(m // tm,),
        in_specs=[pl.BlockSpec((tm, D_MODEL), row), pl.BlockSpec((tm, ODD_MIX), row),
                  pl.BlockSpec((ODD_MIX, D_MODEL), lambda i: (0, 0)),
                  pl.BlockSpec((1, D_MODEL), lambda i: (0, 0)),
                  pl.BlockSpec((1, D_MODEL), lambda i: (0, 0))],
        out_specs=pl.BlockSpec((tm, D_MODEL), row),
        out_shape=jax.ShapeDtypeStruct((m, D_MODEL), F32),
        compiler_params=_cparams("parallel"),
        name="odd_out",
    )(x, y, w, g.reshape(1, -1), b.reshape(1, -1))


def _cmp_combine_body(pt_ref, *refs, n_pages):
    f_refs = refs[:n_pages]
    pe_ref, w1_ref, w2k_ref, w2v_ref, o_ref = refs[n_pages:]
    f = jnp.concatenate([r[0] for r in f_refs], axis=0)
    n = f.shape[0]
    slab = jnp.zeros((n, 128), F32)
    for part, w2_ref in ((0, w2k_ref), (1, w2v_ref)):
        first = f[:, 256 * part:256 * part + 128]
        second = f[:, 256 * part + 128:256 * part + 256]
        pe_term = _dot(pe_ref[part], w1_ref[part])[0:1]
        hid = first + pltpu.roll(second, n - 1, 0) + pe_term
        slab = slab + _dot(jax.nn.gelu(hid).astype(BF16), w2_ref[...])
    o_ref[0] = slab.astype(BF16)


def cmp_combine(f_pages, table, pe, w1, w2k, w2v):
    n_seq, n_pages = table.shape
    idx = lambda i, s, pt: (pt[s * n_pages + i], 0, 0)
    full3 = lambda s, pt: (0, 0, 0)
    grid_spec = pltpu.PrefetchScalarGridSpec(
        num_scalar_prefetch=1,
        grid=(n_seq,),
        in_specs=[pl.BlockSpec((1, 8, 512), functools.partial(idx, i)) for i in range(n_pages)] + [
            pl.BlockSpec((2, 8, 2048), full3), pl.BlockSpec((2, 2048, 128), full3),
            pl.BlockSpec((128, 128), lambda s, pt: (0, 0)), pl.BlockSpec((128, 128), lambda s, pt: (0, 0))],
        out_specs=pl.BlockSpec((1, 8 * n_pages, 128), lambda s, pt: (s, 0, 0)),
    )
    return pl.pallas_call(
        functools.partial(_cmp_combine_body, n_pages=n_pages),
        grid_spec=grid_spec,
        out_shape=jax.ShapeDtypeStruct((n_seq, 8 * n_pages, 128), BF16),
        compiler_params=_cparams("arbitrary"),
        name="cmp_combine",
    )(table.reshape(-1), *([f_pages] * n_pages), pe, w1, w2k, w2v)


def _softmax_step(s_all, mask, distf, slopes, m_ref, l_ref, n_groups, tq, row0=0):
    nr = n_groups * tq
    tk = s_all.shape[1]
    rows = slice(row0, row0 + nr)
    m_prev = m_ref[rows]
    l_prev = l_ref[rows]
    m_out, l_out, a_out, p_out = [], [], [], []
    for r in range(n_groups):
        sub = slice(r * tq, (r + 1) * tq)
        s = s_all[sub]
        if slopes is not None:
            s = s - slopes[r] * distf
        s = jnp.where(mask, s, MASKED)
        m_new = jnp.maximum(m_prev[sub], jnp.broadcast_to(jnp.max(s, axis=1, keepdims=True), (tq, LANES)))
        a = jnp.exp(m_prev[sub] - m_new)
        ps = [jnp.exp(s[:, c:c + LANES] - m_new) for c in range(0, tk, LANES)]
        l_new = a * l_prev[sub] + ps[0]
        for pc in ps[1:]:
            l_new = l_new + pc
        m_out.append(m_new)
        l_out.append(l_new)
        a_out.append(a)
        p_out.append(jnp.concatenate([pc.astype(BF16) for pc in ps], axis=1))
    cat = lambda xs: xs[0] if len(xs) == 1 else jnp.concatenate(xs, axis=0)
    m_ref[rows] = cat(m_out)
    l_ref[rows] = cat(l_out)
    return cat(a_out), cat(p_out)


def _rescale(a, acc):
    w = acc.shape[1]
    if w <= LANES:
        return acc * a[:, 0:w]
    return jnp.concatenate([acc[:, c:c + LANES] * a for c in range(0, w, LANES)], axis=1)


def _inv_or_zero(l):
    return jnp.where(l > 0.0, 1.0 / l, 0.0)


def _finish_mla(acc, l, wuv_ref, o_ref, tq):
    ol = (acc * _inv_or_zero(l)).astype(BF16)
    out = _dot(ol[0:tq], wuv_ref[0])
    for h in range(1, MLA_HEADS):
        out = out + _dot(ol[h * tq:(h + 1) * tq], wuv_ref[h])
    o_ref[...] = out


def _finish_nsa(acc, l, gate, branch, o_ref, tq):
    o = acc * _inv_or_zero(l)
    if o.shape[1] == NSA_DK:
        o = jnp.concatenate([jnp.zeros_like(o), o], axis=1)
    for h in range(NSA_HEADS):
        c = GATE_LANE0 + NSA_HEADS * branch + h
        o_ref[h] = o[h * tq:(h + 1) * tq] * gate[:, c:c + 1]


def _finish_diff(acc, l, lam_ref, sub_ref, lam_init, o_ref, tq, col0=0):
    lp = lam_ref[...]
    lam = (jnp.exp(jnp.sum(lp[0:1] * lp[1:2], axis=1, keepdims=True))
           - jnp.exp(jnp.sum(lp[2:3] * lp[3:4], axis=1, keepdims=True)) + lam_init)
    o = acc * _inv_or_zero(l)
    half = DIFF_REP * tq
    d = o[0:half] - lam * o[half:2 * half]
    d = _rms_norm(d, sub_ref[...]) * (1.0 - lam_init)
    for r in range(DIFF_REP):
        o_ref[:, col0 + 128 * r:col0 + 128 * r + 128] = d[r * tq:(r + 1) * tq]


def _alibi(n_heads, h):
    return 2.0 ** (-8.0 * (h + 1) / n_heads)


def _prompt_flash_body(qi_ref, kj_ref, fl_ref, ll_ref, *refs, mode, tq, tk, lam_init):
    refs = list(refs)
    q_ref = refs.pop(0)
    k_ref = refs.pop(0)
    v_ref = refs.pop(0) if mode == "diff" else None
    if mode == "slc":
        sel_ref = refs.pop(0)
        e_ref = refs.pop(0)
    if mode in ("slc", "win"):
        gate_ref = refs.pop(0)
    if mode == "mla":
        wuv_ref = refs.pop(0)
    if mode == "diff":
        lam_ref = refs.pop(0)
        sub_ref = refs.pop(0)
    o_ref, qb_ref, m_ref, l_ref, acc_ref = refs
    n_groups = q_ref.shape[1]
    g = pl.program_id(1)
    s = pl.program_id(2)

    @pl.when(fl_ref[s] == 1)
    def _():
        qb_ref[...] = q_ref[0].reshape(n_groups * tq, q_ref.shape[3]).astype(BF16)
        m_ref[...] = jnp.full_like(m_ref, M_INIT)
        l_ref[...] = jnp.zeros_like(l_ref)
        acc_ref[...] = jnp.zeros_like(acc_ref)

    kb = k_ref[...].astype(BF16)
    s_all = _dot_nt(qb_ref[...], kb)
    q_pos = qi_ref[s] * tq + lax.broadcasted_iota(jnp.int32, (tq, tk), 0)
    k_pos = kj_ref[s] * tk + lax.broadcasted_iota(jnp.int32, (tq, tk), 1)
    dist = q_pos - k_pos
    mask = dist >= 0
    if mode == "win":
        mask = jnp.logical_and(mask, dist <= WINDOW)
    if mode == "slc":
        picked = _dot(sel_ref[...].astype(BF16), e_ref[...])
        mask = jnp.logical_and(mask, picked > 0.5)
    if mode == "mla":
        slopes = None
    elif mode == "diff":
        s0 = [_alibi(DIFF_HEADS, r) for r in range(DIFF_REP)] * 2
        s1 = [_alibi(DIFF_HEADS, DIFF_REP + r) for r in range(DIFF_REP)] * 2
        slopes = [jnp.where(g == 0, a, b) for a, b in zip(s0, s1)]
    else:
        slopes = [_alibi(NSA_HEADS, h) for h in range(NSA_HEADS)]
    a, p = _softmax_step(s_all, mask, dist.astype(F32), slopes, m_ref, l_ref, n_groups, tq)
    vb = kb[:, 0:128] if mode != "diff" else v_ref[...].astype(BF16)
    acc_ref[...] = _rescale(a, acc_ref[...]) + _dot(p, vb)

    @pl.when(ll_ref[s] == 1)
    def _():
        acc, l = acc_ref[...], jnp.sum(l_ref[...], axis=1, keepdims=True)
        if mode == "mla":
            _finish_mla(acc, l, wuv_ref, o_ref, tq)
        elif mode == "diff":
            _finish_diff(acc, l, lam_ref, sub_ref, lam_init, o_ref, tq)
        else:
            _finish_nsa(acc, l, gate_ref[...], 1 if mode == "slc" else 2, o_ref, tq)


def _prompt_steps(seq, tq, tk, mode):
    qi, kj, fl, ll = [], [], [], []
    for i in range(seq // tq):
        hi = (i * tq + tq - 1) // tk
        lo = max(0, (i * tq - WINDOW) // tk) if mode == "win" else 0
        for j in range(lo, hi + 1):
            qi.append(i); kj.append(j); fl.append(int(j == lo)); ll.append(int(j == hi))
    return [jnp.asarray(np.asarray(a, np.int32)) for a in (qi, kj, fl, ll)]


def prompt_flash(mode, batch, seq, q, k, v=None, sel=None, emat=None, gate=None, wuv=None,
                 lam=None, sub=None, lam_init=0.0):
    n_g, n_groups, m, dk = q.shape
    tq, tk = (256, 256) if mode == "win" else (128, 2048)
    tq, tk = min(tq, seq), min(tk, seq)
    nq, nk = seq // tq, seq // tk
    steps = _prompt_steps(seq, tq, tk, mode)
    n_steps = int(steps[0].shape[0])
    qrow = lambda b, g, s, qi, kj, fl, ll: (b * nq + qi[s], 0)
    kcol = {"mla": 0, "slc": 1, "win": 0, "diff": None}[mode]
    kw = 256 if mode == "mla" else 128
    if mode == "diff":
        kmap = lambda b, g, s, qi, kj, fl, ll: (b * nk + kj[s], g)
    else:
        kmap = lambda b, g, s, qi, kj, fl, ll: (b * nk + kj[s], kcol)
    in_specs = [pl.BlockSpec((1, n_groups, tq, dk), lambda b, g, s, qi, kj, fl, ll: (g, 0, b * nq + qi[s], 0)),
                pl.BlockSpec((tk, kw), kmap)]
    args = [q, k]
    if mode == "diff":
        in_specs.append(pl.BlockSpec((tk, 128), lambda b, g, s, qi, kj, fl, ll: (b * nk + kj[s], 2 + g)))
        args.append(k)
    if mode == "slc":
        in_specs += [pl.BlockSpec((tq, 128), qrow),
                     pl.BlockSpec((128, tk), lambda b, g, s, qi, kj, fl, ll: (0, kj[s]))]
        args += [sel, emat]
    if mode in ("slc", "win"):
        in_specs.append(pl.BlockSpec((tq, 128), qrow))
        args.append(gate)
    if mode == "mla":
        in_specs.append(pl.BlockSpec((MLA_HEADS, 128, 512), lambda b, g, s, qi, kj, fl, ll: (0, 0, 0)))
        args.append(wuv)
    if mode == "diff":
        in_specs += [pl.BlockSpec((4, DIFF_D), lambda b, g, s, qi, kj, fl, ll: (0, 0)),
                     pl.BlockSpec((1, 128), lambda b, g, s, qi, kj, fl, ll: (0, 0))]
        args += [lam, sub]
    rows = batch * seq
    if mode == "mla":
        out_spec = pl.BlockSpec((tq, 512), qrow)
        out_shape = jax.ShapeDtypeStruct((rows, 512), F32)
    elif mode == "diff":
        out_spec = pl.BlockSpec((tq, 512), lambda b, g, s, qi, kj, fl, ll: (b * nq + qi[s], g))
        out_shape = jax.ShapeDtypeStruct((rows, ODD_MIX), F32)
    else:
        out_spec = pl.BlockSpec((NSA_HEADS, tq, 128), lambda b, g, s, qi, kj, fl, ll: (0, b * nq + qi[s], 0))
        out_shape = jax.ShapeDtypeStruct((NSA_HEADS, rows, 128), F32)
    nr = n_groups * tq
    grid_spec = pltpu.PrefetchScalarGridSpec(
        num_scalar_prefetch=4,
        grid=(batch, n_g, n_steps),
        in_specs=in_specs,
        out_specs=out_spec,
        scratch_shapes=[pltpu.VMEM((nr, dk), BF16), pltpu.VMEM((nr, LANES), F32), pltpu.VMEM((nr, LANES), F32),
                        pltpu.VMEM((nr, LANES), F32)],
    )
    return pl.pallas_call(
        functools.partial(_prompt_flash_body, mode=mode, tq=tq, tk=tk, lam_init=lam_init),
        grid_spec=grid_spec,
        out_shape=out_shape,
        compiler_params=_cparams("parallel", "parallel", "arbitrary"),
        name="prompt_flash_" + mode,
    )(*steps, *args)


def _cmp_select_body(q_ref, c_ref, cmap_ref, gate_ref, oc_ref, sel_ref, *, tq, n_cmp, pos0_fn, k_eff):
    nc = c_ref.shape[1]
    qb = q_ref[0].reshape(NSA_HEADS * tq, 128).astype(BF16)
    cb = c_ref[0]
    s_all = _dot_nt(qb, cb)
    pos0 = pos0_fn()
    col = lax.broadcasted_iota(jnp.int32, (tq, nc), 1)
    q_pos = pos0 + lax.broadcasted_iota(jnp.int32, (tq, nc), 0)
    dist = q_pos - (col * CMP_STRIDE + (CMP_LEN - 1))
    mask = jnp.logical_and(dist >= 0, col < n_cmp)
    distf = dist.astype(F32)
    gate = gate_ref[...]
    psum = jnp.zeros((tq, nc), F32)
    for h in range(NSA_HEADS):
        s = s_all[h * tq:(h + 1) * tq] - _alibi(NSA_HEADS, h) * distf
        s = jnp.where(mask, s, MASKED)
        e = jnp.where(mask, jnp.exp(s - jnp.max(s, axis=1, keepdims=True)), 0.0)
        p = e * _inv_or_zero(jnp.sum(e, axis=1, keepdims=True))
        psum = psum + p
        c = GATE_LANE0 + h
        oc_ref[h] = _dot(p.astype(BF16), cb) * gate[:, c:c + 1]
    cmap = cmap_ref[...]
    hi = psum.astype(BF16)
    rem = psum - hi.astype(F32)
    mid = rem.astype(BF16)
    lo = (rem - mid.astype(F32)).astype(BF16)
    imp = _dot(hi, cmap) + _dot(mid, cmap) + _dot(lo, cmap)
    blk = lax.broadcasted_iota(jnp.int32, (tq, LANES), 1)
    qp = pos0 + lax.broadcasted_iota(jnp.int32, (tq, LANES), 0)
    cur = qp // SLC_LEN
    forced = (blk == 0) | (blk == cur) | (blk == cur - 1)
    valid = blk * SLC_LEN <= qp
    v = jnp.where(valid, jnp.where(forced, FORCED_SCORE, imp), MASKED)
    sel = jnp.zeros((tq, LANES), F32)
    blkf = blk.astype(F32)
    for _ in range(k_eff):
        top = jnp.max(v, axis=1, keepdims=True)
        first = jnp.min(jnp.where(v == top, blkf, float(LANES)), axis=1, keepdims=True)
        hit = blkf == first
        sel = jnp.where(hit, 1.0, sel)
        v = jnp.where(hit, -jnp.inf, v)
    sel_ref[...] = jnp.where(valid, sel, 0.0)


def cmp_select(qn, cslab, cmap, gate, *, rows0, n_rows, tq, blocks_per_seq, n_cmp, pos_fixed, k_eff):
    nc = cslab.shape[1]
    nblk = n_rows // tq
    blk0 = rows0 // tq
    if pos_fixed is None:
        pos0_fn = lambda: (pl.program_id(0) % blocks_per_seq) * tq
    else:
        pos0_fn = lambda: pos_fixed
    return pl.pallas_call(
        functools.partial(_cmp_select_body, tq=tq, n_cmp=n_cmp, pos0_fn=pos0_fn, k_eff=k_eff),
        grid=(nblk,),
        in_specs=[pl.BlockSpec((1, NSA_HEADS, tq, 128), lambda i: (0, 0, blk0 + i, 0)),
                  pl.BlockSpec((1, nc, 128), lambda i: (i // blocks_per_seq, 0, 0)),
                  pl.BlockSpec((nc, 128), lambda i: (0, 0)),
                  pl.BlockSpec((tq, 128), lambda i: (blk0 + i, 0))],
        out_specs=[pl.BlockSpec((NSA_HEADS, tq, 128), lambda i: (0, i, 0)),
                   pl.BlockSpec((tq, 128), lambda i: (i, 0))],
        out_shape=[jax.ShapeDtypeStruct((NSA_HEADS, n_rows, 128), F32),
                   jax.ShapeDtypeStruct((n_rows, 128), F32)],
        compiler_params=_cparams("parallel"),
        name="cmp_select",
    )(qn, cslab, cmap, gate)


def _cmp_paged_body(pt_ref, *refs, pp):
    page_refs = refs[:pp]
    w_ref, o_ref, sk_ref, sv_ref = refs[pp:]
    for i, p in enumerate(page_refs):
        sk_ref[i * PAGE_SIZE:(i + 1) * PAGE_SIZE, :] = p[0, 0].T
        sv_ref[i * PAGE_SIZE:(i + 1) * PAGE_SIZE, :] = p[0, 1].T
    for part, s_ref in ((0, sk_ref), (1, sv_ref)):
        acc = jnp.zeros((8 * pp, 2 * CMP_HIDDEN), F32)
        for r in range(CMP_STRIDE):
            x = s_ref[pl.ds(r, 8 * pp, stride=CMP_STRIDE), :]
            acc = acc + _dot(x.astype(BF16), w_ref[part, r])
        o_ref[:, 256 * part:256 * part + 256] = acc


def cmp_paged(pages, table, w, pp):
    n_seq, n_pages = table.shape
    n_st = n_pages // pp

    def page_map(i, sq, st, pt):
        return (pt[sq * n_pages + st * pp + i], 0, 0, 0)

    grid_spec = pltpu.PrefetchScalarGridSpec(
        num_scalar_prefetch=1,
        grid=(n_seq, n_st),
        in_specs=[pl.BlockSpec((1, 2, NSA_DK, PAGE_SIZE), functools.partial(page_map, i)) for i in range(pp)] + [
            pl.BlockSpec((2, CMP_STRIDE, NSA_DK, 2 * CMP_HIDDEN), lambda sq, st, pt: (0, 0, 0, 0))],
        out_specs=pl.BlockSpec((8 * pp, 512), lambda sq, st, pt: (sq * n_st + st, 0)),
        scratch_shapes=[pltpu.VMEM((pp * PAGE_SIZE, NSA_DK), F32), pltpu.VMEM((pp * PAGE_SIZE, NSA_DK), F32)],
    )
    return pl.pallas_call(
        functools.partial(_cmp_paged_body, pp=pp),
        grid_spec=grid_spec,
        out_shape=jax.ShapeDtypeStruct((n_seq * n_pages * 8, 512), F32),
        compiler_params=_cparams("parallel", "arbitrary"),
        name="cmp_paged",
    )(table.reshape(-1), *([pages] * pp), w)


def _stack_pages(mode, page_refs, g, dq):
    if mode == "mla":
        kt = jnp.concatenate([r[0] for r in page_refs], axis=1)
        kt = jnp.concatenate([kt, jnp.zeros((dq - kt.shape[0], kt.shape[1]), F32)], axis=0).astype(BF16)
        return kt, kt[0:MLA_KV_RANK], True
    if mode in ("slc", "win"):
        return (jnp.concatenate([r[0, 0] for r in page_refs], axis=1).astype(BF16),
                jnp.concatenate([r[0, 1] for r in page_refs], axis=1).astype(BF16), True)
    return (jnp.concatenate([r[0, pl.ds(g, PAGE_SIZE, stride=4), :] for r in page_refs], axis=0).astype(BF16),
            jnp.concatenate([r[0, pl.ds(2 + g, PAGE_SIZE, stride=4), :] for r in page_refs], axis=0).astype(BF16),
            False)


def _sample_flash_body(pt_ref, *refs, mode, pp, ptok, t_new, past, kpos0, lam_init):
    refs = list(refs)
    q_ref = refs.pop(0)
    page_refs = [refs.pop(0) for _ in range(pp)]
    kn_ref = refs.pop(0)
    if mode == "slc":
        sel_ref = refs.pop(0)
        e_ref = refs.pop(0)
    if mode in ("slc", "win"):
        gate_ref = refs.pop(0)
    if mode == "mla":
        wuv_ref = refs.pop(0)
    if mode == "diff":
        lam_ref = refs.pop(0)
        sub_ref = refs.pop(0)
    o_ref, qb_ref, m_ref, l_ref, acc_ref, kpad_ref, vpad_ref = refs
    n_g, n_groups = q_ref.shape[0], q_ref.shape[1]
    tq = t_new
    nr = n_groups * tq
    st = pl.program_id(1)
    n_st = pl.num_programs(1)
    tk = pp * ptok
    dq = qb_ref.shape[1]

    @pl.when(st == 0)
    def _():
        q = q_ref[...].reshape(n_g * nr, q_ref.shape[3])
        qb_ref[...] = q[:, 0:dq].astype(BF16)
        m_ref[...] = jnp.full_like(m_ref, M_INIT)
        l_ref[...] = jnp.zeros_like(l_ref)
        acc_ref[...] = jnp.zeros_like(acc_ref)

    assert tq & (tq - 1) == 0
    head_col = lax.broadcasted_iota(jnp.int32, (nr, 1), 0) >> (tq.bit_length() - 1)

    def slopes_of(g):
        if mode == "mla":
            return None
        col = jnp.zeros((nr, 1), F32)
        for r in range(n_groups):
            if mode == "diff":
                sl = _alibi(DIFF_HEADS, g * DIFF_REP + r % DIFF_REP)
            else:
                sl = _alibi(NSA_HEADS, r)
            col = jnp.where(head_col == r, sl, col)
        return [col]

    def dist_and_mask(width, k_pos0, k_limit):
        row = lax.broadcasted_iota(jnp.int32, (nr, width), 0)
        colk = lax.broadcasted_iota(jnp.int32, (nr, width), 1)
        dist = (past + (row & (tq - 1))) - (k_pos0 + colk)
        mask = dist >= 0
        if k_limit is not None:
            mask = jnp.logical_and(mask, colk < k_limit)
        if mode == "win":
            mask = jnp.logical_and(mask, dist <= WINDOW)
        return dist, mask

    dist, mask = dist_and_mask(tk, kpos0 + st * tk, None)
    if mode == "slc":
        picked = _dot(sel_ref[...].astype(BF16), e_ref[...])
        mask = jnp.logical_and(mask, jnp.concatenate([picked] * n_groups, axis=0) > 0.5)
    distf = dist.astype(F32)
    for g in range(n_g):
        rows = slice(g * nr, (g + 1) * nr)
        qb = qb_ref[rows]
        kb, vb, tokens_last = _stack_pages(mode, page_refs, g, dq)
        s_all = _dot(qb, kb) if tokens_last else _dot_nt(qb, kb)
        a, p = _softmax_step(s_all, mask, distf, slopes_of(g), m_ref, l_ref, 1, nr, row0=g * nr)
        pv = _dot_nt(p, vb) if tokens_last else _dot(p, vb)
        acc_ref[rows] = _rescale(a, acc_ref[rows]) + pv

    @pl.when(st == n_st - 1)
    def _():
        dn, mn = dist_and_mask(LANES, past, t_new)
        dnf = dn.astype(F32)
        kn = kn_ref[...]
        for g in range(n_g):
            rows = slice(g * nr, (g + 1) * nr)
            if mode == "diff":
                k_new, v_new = kn[:, 128 * g:128 * g + 128], kn[:, 256 + 128 * g:384 + 128 * g]
            elif mode == "mla":
                k_new, v_new = kn, kn[:, 0:128]
            else:
                k_new, v_new = kn[:, 0:NSA_DK], kn[:, NSA_DK:2 * NSA_DK]
            kpad_ref[...] = jnp.zeros_like(kpad_ref)
            kpad_ref[0:t_new, :] = k_new.astype(BF16)
            vpad_ref[...] = jnp.zeros_like(vpad_ref)
            vpad_ref[0:t_new, :] = v_new.astype(BF16)
            s_new = _dot_nt(qb_ref[rows], kpad_ref[...])
            a, p = _softmax_step(s_new, mn, dnf, slopes_of(g), m_ref, l_ref, 1, nr, row0=g * nr)
            acc = _rescale(a, acc_ref[rows]) + _dot(p, vpad_ref[...])
            l = jnp.sum(l_ref[rows], axis=1, keepdims=True)
            if mode == "mla":
                _finish_mla(acc, l, wuv_ref, o_ref, tq)
            elif mode == "diff":
                _finish_diff(acc, l, lam_ref, sub_ref, lam_init, o_ref, tq, col0=512 * g)
            else:
                _finish_nsa(acc, l, gate_ref[...], 1 if mode == "slc" else 2, o_ref, tq)


def sample_flash(mode, n_seq, t_new, row0, past, q, pages, table, knew, kpos0, pp, sel=None, emat=None,
                 gate=None, wuv=None, lam=None, sub=None, lam_init=0.0):
    n_g, n_groups, m, dk = q.shape
    n_pages = table.shape[1]
    assert n_pages % pp == 0 and row0 % t_new == 0
    n_st = n_pages // pp
    blk0 = row0 // t_new
    kn_w, kn_col = {"mla": (256, 0), "slc": (128, 1), "win": (128, 0), "diff": (512, 0)}[mode]
    dq, dkk, dv = {"mla": (256, 256, 128), "slc": (64, 64, 64), "win": (64, 64, 64), "diff": (128, 128, 128)}[mode]
    if mode in ("slc", "win"):
        page_block = (1, 2) + pages.shape[2:]
        ptok = pages.shape[3]
    else:
        page_block = (1,) + pages.shape[1:]
        ptok = PAGE_SIZE
    part_blk = 1 if mode == "slc" else 0

    def page_map(i, sq, st, pt):
        return (pt[sq * n_pages + st * pp + i], part_blk) + (0,) * (len(page_block) - 2)

    in_specs = [pl.BlockSpec((n_g, n_groups, t_new, dk), lambda sq, st, pt: (0, 0, blk0 + sq, 0))]
    args = [q]
    in_specs += [pl.BlockSpec(page_block, functools.partial(page_map, i)) for i in range(pp)]
    args += [pages] * pp
    in_specs.append(pl.BlockSpec((t_new, kn_w), lambda sq, st, pt: (blk0 + sq, kn_col)))
    args.append(knew)
    if mode == "slc":
        in_specs += [pl.BlockSpec((t_new, 128), lambda sq, st, pt: (sq, 0)),
                     pl.BlockSpec((128, pp * PAGE_SIZE), lambda sq, st, pt: (0, st))]
        args += [sel, emat]
    if mode in ("slc", "win"):
        in_specs.append(pl.BlockSpec((t_new, 128), lambda sq, st, pt: (blk0 + sq, 0)))
        args.append(gate)
    if mode == "mla":
        in_specs.append(pl.BlockSpec((MLA_HEADS, 128, 512), lambda sq, st, pt: (0, 0, 0)))
        args.append(wuv)
    if mode == "diff":
        in_specs += [pl.BlockSpec((4, DIFF_D), lambda sq, st, pt: (0, 0)),
                     pl.BlockSpec((1, 128), lambda sq, st, pt: (0, 0))]
        args += [lam, sub]
    rows = n_seq * t_new
    if mode == "mla":
        out_spec = pl.BlockSpec((t_new, 512), lambda sq, st, pt: (sq, 0))
        out_shape = jax.ShapeDtypeStruct((rows, 512), F32)
    elif mode == "diff":
        out_spec = pl.BlockSpec((t_new, ODD_MIX), lambda sq, st, pt: (sq, 0))
        out_shape = jax.ShapeDtypeStruct((rows, ODD_MIX), F32)
    else:
        out_spec = pl.BlockSpec((NSA_HEADS, t_new, 128), lambda sq, st, pt: (0, sq, 0))
        out_shape = jax.ShapeDtypeStruct((NSA_HEADS, rows, 128), F32)
    nr = n_g * n_groups * t_new
    grid_spec = pltpu.PrefetchScalarGridSpec(
        num_scalar_prefetch=1,
        grid=(n_seq, n_st),
        in_specs=in_specs,
        out_specs=out_spec,
        scratch_shapes=[pltpu.VMEM((nr, dq), BF16), pltpu.VMEM((nr, LANES), F32), pltpu.VMEM((nr, LANES), F32),
                        pltpu.VMEM((nr, dv), F32),
                        pltpu.VMEM((LANES, dkk), BF16), pltpu.VMEM((LANES, dv), BF16)],
    )
    return pl.pallas_call(
        functools.partial(_sample_flash_body, mode=mode, pp=pp, ptok=ptok, t_new=t_new, past=past, kpos0=kpos0,
                          lam_init=lam_init),
        grid_spec=grid_spec,
        out_shape=out_shape,
        compiler_params=_cparams("parallel", "arbitrary"),
        name="sample_flash_" + mode,
    )(table.reshape(-1), *args)


def _even_weights(w_in, gate_b, w_uq, w_uk, w_uv, w_out, cmp_w1, cmp_w2, cmp_pe):
    cuts = np.cumsum(EVEN_SPLITS)[:-1].tolist()
    q_nsa, kv_nsa, kv_win, gate, q_lat, kv_lat, k_rope = jnp.split(w_in, cuts, axis=1)
    pad = jnp.zeros((D_MODEL, EVEN_IN_PAD - sum(EVEN_SPLITS)), F32)
    w = jnp.concatenate([q_nsa, kv_nsa, kv_win, q_lat, kv_lat, k_rope, gate, pad], axis=1).astype(BF16)
    gb = jnp.zeros((1, 128), F32).at[0, GATE_LANE0:GATE_LANE0 + 3 * NSA_HEADS].set(gate_b)
    uq = w_uq.reshape(MLA_Q_RANK, MLA_HEADS, MLA_NOPE + MLA_ROPE)
    wuq = jnp.concatenate([uq[:, :, :MLA_NOPE].reshape(MLA_Q_RANK, -1),
                           uq[:, :, MLA_NOPE:].reshape(MLA_Q_RANK, -1)], axis=1).astype(BF16)
    wuk = jnp.zeros((MLA_HEADS, MLA_NOPE, MLA_HEADS, MLA_KV_RANK), F32)
    wuv = jnp.zeros((MLA_HEADS, MLA_KV_RANK, MLA_HEADS, MLA_V), F32)
    for h in range(MLA_HEADS):
        wuk = wuk.at[h, :, h, :].set(w_uk[h].T)
        wuv = wuv.at[h, :, h, :].set(w_uv[h])
    wuk = wuk.reshape(MLA_HEADS * MLA_NOPE, MLA_HEADS * MLA_KV_RANK).astype(BF16)
    wuv = wuv.reshape(MLA_HEADS, MLA_KV_RANK, MLA_HEADS * MLA_V).astype(BF16)
    n_nsa = NSA_HEADS * NSA_DK
    wn = jnp.concatenate([jnp.zeros((NSA_HEADS, NSA_DK, D_MODEL), F32),
                          w_out[:n_nsa].reshape(NSA_HEADS, NSA_DK, D_MODEL)], axis=1).astype(BF16)
    wm = w_out[n_nsa:].astype(BF16)
    w1r = cmp_w1.reshape(2, 2, CMP_STRIDE, NSA_DK, CMP_HIDDEN)
    wc = jnp.zeros((CMP_STRIDE, 2, NSA_DK, 2, 2, CMP_HIDDEN), F32)
    for part in range(2):
        for half in range(2):
            wc = wc.at[:, part, :, part, half, :].set(w1r[part, half])
    wc = wc.reshape(CMP_STRIDE * 2 * NSA_DK, 4 * CMP_HIDDEN).astype(BF16)
    wp = jnp.transpose(w1r, (0, 2, 3, 1, 4)).reshape(2, CMP_STRIDE, NSA_DK, 2 * CMP_HIDDEN).astype(BF16)
    pe = jnp.zeros((2, 8, CMP_LEN * NSA_DK), F32).at[:, 0, :].set(cmp_pe.reshape(2, -1)).astype(BF16)
    w1 = cmp_w1.reshape(2, CMP_LEN * NSA_DK, CMP_HIDDEN).astype(BF16)
    zeros = jnp.zeros((CMP_HIDDEN, NSA_DK), F32)
    w2k = jnp.concatenate([cmp_w2[0], zeros], axis=1).astype(BF16)
    w2v = jnp.concatenate([zeros, cmp_w2[1]], axis=1).astype(BF16)
    return dict(w=w, gb=gb, wuq=wuq, wuk=wuk, wuv=wuv, wn=wn, wm=wm, wc=wc, wp=wp, pe=pe, w1=w1, w2k=w2k, w2v=w2v)


def _rope_tables(pos):
    half = MLA_ROPE // 2
    inv = ROPE_THETA ** (-jnp.arange(half, dtype=F32) / half)
    ang = pos.astype(F32)[:, None] * inv[None, :]
    cos, sin = jnp.cos(ang), jnp.sin(ang)
    cos = jnp.tile(jnp.concatenate([cos, cos], axis=1), (1, 4))
    sin = jnp.tile(jnp.concatenate([-sin, sin], axis=1), (1, 4))
    return cos, sin


def _cmap(n_rows, n_cmp):
    cs = np.arange(n_rows)[:, None] * CMP_STRIDE
    bs = np.arange(LANES)[None, :] * SLC_LEN
    m = (cs < bs + SLC_LEN) & (cs + CMP_LEN > bs) & (np.arange(n_rows)[:, None] < n_cmp)
    return jnp.asarray(m.astype(np.float32)).astype(BF16)


def _emat(n_keys):
    m = (np.arange(n_keys)[None, :] // SLC_LEN) == np.arange(LANES)[:, None]
    return jnp.asarray(m.astype(np.float32)).astype(BF16)


def kernel(x_prompt, x_sample, cache_nsa, cache_nsa_win, cache_mla, cache_diff, page_table, ln_g, ln_b, ffn_w_in, ffn_w_out, even_w_in, nsa_gate_b, nsa_cmp_pe, nsa_cmp_w1, nsa_cmp_w2, mla_q_norm_g, mla_w_uq, mla_kv_norm_g, mla_w_uk, mla_w_uv, even_w_out, odd_w_in, diff_lambda, diff_subln_g, odd_w_out):
    batch, seq, _ = x_prompt.shape
    n_seq, t_new, _ = x_sample.shape
    n_pages = page_table.shape[1]
    past = n_pages * PAGE_SIZE
    n_pool = cache_nsa.shape[1]
    rows_p = batch * seq
    rows_s = n_seq * t_new
    n_win = cache_nsa_win.shape[2]
    assert seq % 512 == 0 and past % 512 == 0 and n_win % PAGE_SIZE == 0 and t_new == 8

    x = jnp.concatenate([x_prompt.reshape(rows_p, D_MODEL), x_sample.reshape(rows_s, D_MODEL)], axis=0)
    pos = jnp.concatenate([jnp.tile(jnp.arange(seq, dtype=jnp.int32), batch),
                           jnp.tile(past + jnp.arange(t_new, dtype=jnp.int32), n_seq)])
    cos, sin = _rope_tables(pos)
    w_in_bf = ffn_w_in.astype(BF16)
    w_out_bf = ffn_w_out.astype(BF16)
    table = page_table.astype(jnp.int32)
    ident_p = jnp.arange(batch * (seq // PAGE_SIZE), dtype=jnp.int32).reshape(batch, seq // PAGE_SIZE)
    emat_p = _emat(seq)
    emat_s = _emat(past)
    cmap_p = _cmap(seq // CMP_STRIDE, seq // CMP_STRIDE - 1)
    cmap_s = _cmap(past // CMP_STRIDE, past // CMP_STRIDE - 1)
    pages_per_step = lambda cap: max(d for d in range(1, min(cap, n_pages) + 1) if n_pages % d == 0)
    pp = pages_per_step(64)
    pp_diff = pages_per_step(32)
    pp_cmp = pages_per_step(64)
    ident_s = jnp.arange(n_seq * n_pages, dtype=jnp.int32).reshape(n_seq, n_pages)
    nsa_pages = jnp.transpose(cache_nsa, (0, 1, 3, 4, 2)).reshape(-1, 4, NSA_DK, PAGE_SIZE)
    win_pages = jnp.transpose(cache_nsa_win, (0, 1, 3, 4, 2)).reshape(-1, 2, NSA_DK, n_win)
    mla_pages = jnp.transpose(cache_mla, (0, 1, 3, 2)).reshape(-1, MLA_KV_RANK + MLA_ROPE, PAGE_SIZE)
    diff_pages = cache_diff.reshape(-1, 4 * PAGE_SIZE, 2 * DIFF_D)
    ident_w = jnp.arange(n_seq, dtype=jnp.int32).reshape(n_seq, 1)

    outs = {k: [] for k in ("nsa", "win", "mla", "diff")}
    for li in range(DEPTH):
        j = li // 2
        x = ffn_ln(x, w_in_bf[li, 0], w_out_bf[li, 0], ln_g[li, 0], ln_b[li, 0])
        if li % 2 == 0:
            ew = _even_weights(even_w_in[j], nsa_gate_b[j], mla_w_uq[j], mla_w_uk[j], mla_w_uv[j], even_w_out[j],
                               nsa_cmp_w1[j], nsa_cmp_w2[j], nsa_cmp_pe[j])
            kvn, kvw, gate, qn, rows, mk, qm = even_proj(
                x, ew["w"], cos, sin, ew["gb"], mla_q_norm_g[j].reshape(1, -1), mla_kv_norm_g[j].reshape(1, -1),
                ew["wuq"], ew["wuk"])
            qn4 = qn[None]
            qm4 = qm[None]
            chunks_p = kvn[:rows_p, :128].reshape(rows_p // CMP_STRIDE, CMP_STRIDE * 128).astype(BF16)
            f_p = matmul(chunks_p, ew["wc"]).reshape(-1, 8, 512)
            c_p = cmp_combine(f_p, ident_p, ew["pe"], ew["w1"], ew["w2k"], ew["w2v"])
            table_j = table + j * n_pool
            f_s = cmp_paged(nsa_pages, table_j, ew["wp"], pp_cmp).reshape(n_seq * n_pages, 8, 512)
            c_s = cmp_combine(f_s, ident_s, ew["pe"], ew["w1"], ew["w2k"], ew["w2v"])

            oc_p, sel_p = cmp_select(qn4, c_p, cmap_p, gate, rows0=0, n_rows=rows_p, tq=128,
                                     blocks_per_seq=seq // 128, n_cmp=seq // CMP_STRIDE - 1, pos_fixed=None,
                                     k_eff=N_SELECT)
            oc_s, sel_s = cmp_select(qn4, c_s, cmap_s, gate, rows0=rows_p, n_rows=rows_s, tq=t_new,
                                     blocks_per_seq=1, n_cmp=past // CMP_STRIDE - 1, pos_fixed=past,
                                     k_eff=N_SELECT - 1)
            os_p = prompt_flash("slc", batch, seq, qn4, kvn, sel=sel_p, emat=emat_p, gate=gate)
            ow_p = prompt_flash("win", batch, seq, qn4, kvw, gate=gate)
            om_p = prompt_flash("mla", batch, seq, qm4, mk, wuv=ew["wuv"])
            os_s = sample_flash("slc", n_seq, t_new, rows_p, past, qn4, nsa_pages, table_j, kvn, 0, pp,
                                sel=sel_s, emat=emat_s, gate=gate)
            ow_s = sample_flash("win", n_seq, t_new, rows_p, past, qn4, win_pages, ident_w + j * n_seq, kvw,
                                past - n_win, 1, gate=gate)
            om_s = sample_flash("mla", n_seq, t_new, rows_p, past, qm4, mla_pages, table_j, mk, 0, pp,
                                wuv=ew["wuv"])
            cat1 = lambda a, b: jnp.concatenate([a, b], axis=1)
            x = even_out(x, cat1(oc_p, oc_s), cat1(os_p, os_s), cat1(ow_p, ow_s),
                         jnp.concatenate([om_p, om_s], axis=0), ew["wn"], ew["wm"], ln_g[li, 1], ln_b[li, 1])
            outs["nsa"].append(kvn)
            outs["win"].append(kvw)
            outs["mla"].append(rows)
        else:
            lam_init = 0.8 - 0.6 * math.exp(-0.3 * li)
            kv, qd = odd_proj(x, odd_w_in[j].astype(BF16))
            sub = diff_subln_g[j].reshape(1, -1)
            od_p = prompt_flash("diff", batch, seq, qd, kv, lam=diff_lambda[j], sub=sub, lam_init=lam_init)
            od_s = sample_flash("diff", n_seq, t_new, rows_p, past, qd, diff_pages, table + j * n_pool, kv, 0, pp_diff,
                                lam=diff_lambda[j], sub=sub, lam_init=lam_init)
            x = odd_out(x, jnp.concatenate([od_p, od_s], axis=0), odd_w_out[j].astype(BF16),
                        ln_g[li, 1], ln_b[li, 1])
            outs["diff"].append(kv)
        x = ffn_ln(x, w_in_bf[li, 1], w_out_bf[li, 1], ln_g[li, 2], ln_b[li, 2])

    def split(a, shape_p, shape_s):
        return a[:rows_p].reshape(shape_p), a[rows_p:].reshape(shape_s)

    y_p, y_s = split(x, (batch, seq, D_MODEL), (n_seq, t_new, D_MODEL))
    nsa = [split(a, (batch, seq, 4, NSA_DK), (n_seq, t_new, 4, NSA_DK)) for a in outs["nsa"]]
    win = [split(a, (batch, seq, 2, NSA_DK), (n_seq, t_new, 2, NSA_DK)) for a in outs["win"]]
    mla = [split(a, (batch, seq, MLA_KV_RANK + MLA_ROPE), (n_seq, t_new, MLA_KV_RANK + MLA_ROPE))
           for a in outs["mla"]]
    dif = [split(a, (batch, seq, 2, DIFF_KV_HEADS, 2 * DIFF_D), (n_seq, t_new, 2, DIFF_KV_HEADS, 2 * DIFF_D))
           for a in outs["diff"]]
    n_keep = min(WINDOW, seq)
    win_p = jnp.stack([w[0][:, seq - n_keep:] for w in win], 0)
    win_s = jnp.stack([jnp.concatenate([cache_nsa_win[i][:, t_new:], w[1]], axis=1) for i, w in enumerate(win)], 0)
    return (y_p, y_s,
            jnp.stack([a[0] for a in nsa], 0), jnp.stack([a[1] for a in nsa], 0),
            win_p, win_s,
            jnp.stack([a[0] for a in mla], 0), jnp.stack([a[1] for a in mla], 0),
            jnp.stack([a[0] for a in dif], 0), jnp.stack([a[1] for a in dif], 0))
```

```python
import functools
import math

import numpy as np
import jax
import jax.numpy as jnp
from jax import lax
from jax.experimental import pallas as pl
from jax.experimental.pallas import tpu as pltpu

D_MODEL = 1024
DEPTH = 4
PAGE_SIZE = 128
ALPHA = (2.0 * DEPTH) ** 0.25
LN_EPS = 1e-5
RMS_EPS = 1e-6
FFN_HIDDEN = ((8 * D_MODEL // 3 + 127) // 128) * 128
NSA_HEADS = 8
NSA_DK = 64
CMP_LEN = 32
CMP_STRIDE = 16
CMP_HIDDEN = 2 * NSA_DK
SLC_LEN = 64
N_SELECT = 16
WINDOW = 512
FORCED_SCORE = 1e9
MLA_HEADS = 8
MLA_Q_RANK = 256
MLA_KV_RANK = 128
MLA_NOPE = 64
MLA_ROPE = 32
MLA_V = 64
ROPE_THETA = 10000.0
DIFF_HEADS = 8
DIFF_KV_HEADS = 2
DIFF_D = 64
DIFF_REP = DIFF_HEADS // DIFF_KV_HEADS
EVEN_SPLITS = (NSA_HEADS * NSA_DK, 4 * NSA_DK, 2 * NSA_DK, 3 * NSA_HEADS, MLA_Q_RANK, MLA_KV_RANK, MLA_ROPE)
ODD_MIX = DIFF_HEADS * 2 * DIFF_D

LANES = 128
MASKED = -1e30
M_INIT = -1e29
VMEM_LIMIT_BYTES = 56 * 2 ** 20
EVEN_IN_PAD = 1408
GATE_LANE0 = 32
BF16 = jnp.bfloat16
F32 = jnp.float32


def _cparams(*sem):
    return pltpu.CompilerParams(dimension_semantics=sem, vmem_limit_bytes=VMEM_LIMIT_BYTES)


def _pick_tile(n, cap, mult=8):
    t = min(cap, n)
    while t > mult and (n % t or t % mult):
        t -= mult
    assert n % t == 0, (n, cap, mult)
    return t


def _layer_norm(y, g, b):
    mu = jnp.mean(y, axis=-1, keepdims=True)
    d = y - mu
    var = jnp.mean(d * d, axis=-1, keepdims=True)
    return d * lax.rsqrt(var + LN_EPS) * g + b


def _rms_norm(x, g):
    return x * lax.rsqrt(jnp.mean(x * x, axis=-1, keepdims=True) + RMS_EPS) * g


def _dot(a, b):
    return jnp.dot(a, b, preferred_element_type=F32)


def _dot_nt(a, b):
    return lax.dot_general(a, b, (((1,), (1,)), ((), ())), preferred_element_type=F32)


def _ffn_ln_body(x_ref, wg_ref, wu_ref, wo_ref, g_ref, b_ref, o_ref, acc_ref, xb_ref, *, nh):
    h = pl.program_id(1)

    @pl.when(h == 0)
    def _():
        acc_ref[...] = jnp.zeros_like(acc_ref)
        xb_ref[...] = x_ref[...].astype(BF16)

    xb = xb_ref[...]
    gate = _dot(xb, wg_ref[...])
    up = _dot(xb, wu_ref[...])
    act = (gate * jax.nn.sigmoid(gate) * up).astype(BF16)
    acc_ref[...] += _dot(act, wo_ref[...])

    @pl.when(h == nh - 1)
    def _():
        y = ALPHA * x_ref[...] + 0.5 * acc_ref[...]
        o_ref[...] = _layer_norm(y, g_ref[...], b_ref[...])


def ffn_ln(x, w_in, w_out, g, b):
    m = x.shape[0]
    tm = _pick_tile(m, 1024)
    th = 256
    nh = FFN_HIDDEN // th
    return pl.pallas_call(
        functools.partial(_ffn_ln_body, nh=nh),
        grid=(m // tm, nh),
        in_specs=[
            pl.BlockSpec((tm, D_MODEL), lambda i, h: (i, 0)),
            pl.BlockSpec((D_MODEL, th), lambda i, h: (0, h)),
            pl.BlockSpec((D_MODEL, th), lambda i, h: (0, h + nh)),
            pl.BlockSpec((th, D_MODEL), lambda i, h: (h, 0)),
            pl.BlockSpec((1, D_MODEL), lambda i, h: (0, 0)),
            pl.BlockSpec((1, D_MODEL), lambda i, h: (0, 0)),
        ],
        out_specs=pl.BlockSpec((tm, D_MODEL), lambda i, h: (i, 0)),
        out_shape=jax.ShapeDtypeStruct((m, D_MODEL), F32),
        scratch_shapes=[pltpu.VMEM((tm, D_MODEL), F32), pltpu.VMEM((tm, D_MODEL), BF16)],
        compiler_params=_cparams("parallel", "arbitrary"),
        name="ffn_ln",
    )(x, w_in, w_in, w_out, g.reshape(1, -1), b.reshape(1, -1))


def _mm_body(x_ref, w_ref, o_ref):
    o_ref[...] = _dot(x_ref[...], w_ref[...])


def matmul(x, w):
    m, k = x.shape
    n = w.shape[1]
    tm = _pick_tile(m, 512, 16)
    return pl.pallas_call(
        _mm_body,
        grid=(m // tm,),
        in_specs=[pl.BlockSpec((tm, k), lambda i: (i, 0)), pl.BlockSpec((k, n), lambda i: (0, 0))],
        out_specs=pl.BlockSpec((tm, n), lambda i: (i, 0)),
        out_shape=jax.ShapeDtypeStruct((m, n), F32),
        compiler_params=_cparams("parallel"),
        name="cmp_matmul",
    )(x, w)


def _even_proj_body(x_ref, w_ref, cos_ref, sin_ref, gb_ref, gq_ref, gkv_ref, wuq_ref, wuk_ref,
                    kvn_ref, kvw_ref, gate_ref, qn_ref, rows_ref, mk_ref, qm_ref):
    tm = x_ref.shape[0]
    z = _dot(x_ref[...].astype(BF16), w_ref[...])
    kvn_ref[...] = z[:, 512:768]
    kvw_ref[...] = z[:, 768:896]
    lane = lax.broadcasted_iota(jnp.int32, (tm, LANES), 1)
    low_half = lane < 64
    nsa_scale = NSA_DK ** -0.5
    for h in range(NSA_HEADS):
        grp = z[:, 128 * (h // 2):128 * (h // 2) + 128]
        if h % 2:
            grp = pltpu.roll(grp, 64, 1)
        qn_ref[h] = jnp.where(low_half, grp * nsa_scale, 0.0)
    slab = z[:, 1280:1408]
    gate_ref[...] = jax.nn.sigmoid(slab + gb_ref[...])
    cos = cos_ref[...]
    sin = sin_ref[...]
    first_half = (lane & 31) < 16

    def rope(v):
        swapped = jnp.where(first_half, pltpu.roll(v, LANES - 16, 1), pltpu.roll(v, 16, 1))
        return v * cos + swapped * sin

    kr = rope(slab)
    c = _rms_norm(z[:, 1152:1280], gkv_ref[...])
    rows_ref[:, 0:128] = c
    rows_ref[:, 128:160] = kr[:, 0:32]
    mk_ref[:, 0:128] = c
    mk_ref[:, 128:256] = jnp.where(lane < 32, kr, 0.0)
    ql = _rms_norm(z[:, 896:1152], gq_ref[...])
    qm = _dot(ql.astype(BF16), wuq_ref[...])
    qabs = _dot(qm[:, 0:512].astype(BF16), wuk_ref[...])
    r0 = rope(qm[:, 512:640])
    r1 = rope(qm[:, 640:768])
    mla_scale = (MLA_NOPE + MLA_ROPE) ** -0.5
    for h in range(MLA_HEADS):
        qm_ref[h, :, 0:128] = qabs[:, 128 * h:128 * h + 128] * mla_scale
        rr = r0 if h < 4 else r1
        if h % 4:
            rr = pltpu.roll(rr, LANES - 32 * (h % 4), 1)
        qm_ref[h, :, 128:256] = jnp.where(lane < 32, rr * mla_scale, 0.0)


def even_proj(x, w, cos, sin, gb, gq, gkv, wuq, wuk):
    m = x.shape[0]
    tm = _pick_tile(m, 512)
    row = lambda i: (i, 0)
    full2 = lambda i: (0, 0)
    out_shapes = [
        jax.ShapeDtypeStruct((m, 256), F32),
        jax.ShapeDtypeStruct((m, 128), F32),
        jax.ShapeDtypeStruct((m, 128), F32),
        jax.ShapeDtypeStruct((NSA_HEADS, m, 128), F32),
        jax.ShapeDtypeStruct((m, 160), F32),
        jax.ShapeDtypeStruct((m, 256), F32),
        jax.ShapeDtypeStruct((MLA_HEADS, m, 256), F32),
    ]
    out_specs = [
        pl.BlockSpec((tm, 256), row), pl.BlockSpec((tm, 128), row), pl.BlockSpec((tm, 128), row),
        pl.BlockSpec((NSA_HEADS, tm, 128), lambda i: (0, i, 0)),
        pl.BlockSpec((tm, 160), row), pl.BlockSpec((tm, 256), row),
        pl.BlockSpec((MLA_HEADS, tm, 256), lambda i: (0, i, 0)),
    ]
    return pl.pallas_call(
        _even_proj_body,
        grid=(m // tm,),
        in_specs=[
            pl.BlockSpec((tm, D_MODEL), row),
            pl.BlockSpec((D_MODEL, EVEN_IN_PAD), full2),
            pl.BlockSpec((tm, 128), row), pl.BlockSpec((tm, 128), row),
            pl.BlockSpec((1, 128), full2), pl.BlockSpec((1, 256), full2), pl.BlockSpec((1, 128), full2),
            pl.BlockSpec((MLA_Q_RANK, 768), full2),
            pl.BlockSpec((512, 1024), full2),
        ],
        out_specs=out_specs,
        out_shape=out_shapes,
        compiler_params=_cparams("parallel"),
        name="even_proj",
    )(x, w, cos, sin, gb, gq, gkv, wuq, wuk)


def _odd_proj_body(x_ref, w_ref, kv_ref, q_ref):
    tm = x_ref.shape[0]
    z = _dot(x_ref[...].astype(BF16), w_ref[...])
    kv_ref[...] = z[:, ODD_MIX:]
    lane = lax.broadcasted_iota(jnp.int32, (tm, LANES), 1)
    scale = DIFF_D ** -0.5
    for g in range(DIFF_KV_HEADS):
        for mm in range(2):
            keep = (lane < 64) if mm == 0 else (lane >= 64)
            for r in range(DIFF_REP):
                hd = g * DIFF_REP + r
                q_ref[g, mm * DIFF_REP + r] = jnp.where(keep, z[:, 128 * hd:128 * hd + 128] * scale, 0.0)


def odd_proj(x, w):
    m = x.shape[0]
    tm = _pick_tile(m, 512)
    return pl.pallas_call(
        _odd_proj_body,
        grid=(m // tm,),
        in_specs=[pl.BlockSpec((tm, D_MODEL), lambda i: (i, 0)),
                  pl.BlockSpec((D_MODEL, w.shape[1]), lambda i: (0, 0))],
        out_specs=[pl.BlockSpec((tm, 512), lambda i: (i, 0)),
                   pl.BlockSpec((DIFF_KV_HEADS, 2 * DIFF_REP, tm, 128), lambda i: (0, 0, i, 0))],
        out_shape=[jax.ShapeDtypeStruct((m, 512), F32),
                   jax.ShapeDtypeStruct((DIFF_KV_HEADS, 2 * DIFF_REP, m, 128), F32)],
        compiler_params=_cparams("parallel"),
        name="odd_proj",
    )(x, w)


def _even_out_body(x_ref, oc_ref, os_ref, ow_ref, om_ref, wn_ref, wm_ref, g_ref, b_ref, o_ref):
    y = _dot(om_ref[...].astype(BF16), wm_ref[...])
    for h in range(NSA_HEADS):
        y = y + _dot((oc_ref[h] + os_ref[h] + ow_ref[h]).astype(BF16), wn_ref[h])
    o_ref[...] = _layer_norm(ALPHA * x_ref[...] + y, g_ref[...], b_ref[...])


def even_out(x, oc, osl, ow, om, wn, wm, g, b):
    m = x.shape[0]
    tm = _pick_tile(m, 512)
    row = lambda i: (i, 0)
    hspec = pl.BlockSpec((NSA_HEADS, tm, 128), lambda i: (0, i, 0))
    return pl.pallas_call(
        _even_out_body,
        grid=(m // tm,),
        in_specs=[pl.BlockSpec((tm, D_MODEL), row), hspec, hspec, hspec,
                  pl.BlockSpec((tm, 512), row),
                  pl.BlockSpec((NSA_HEADS, 128, D_MODEL), lambda i: (0, 0, 0)),
                  pl.BlockSpec((512, D_MODEL), lambda i: (0, 0)),
                  pl.BlockSpec((1, D_MODEL), lambda i: (0, 0)),
                  pl.BlockSpec((1, D_MODEL), lambda i: (0, 0))],
        out_specs=pl.BlockSpec((tm, D_MODEL), row),
        out_shape=jax.ShapeDtypeStruct((m, D_MODEL), F32),
        compiler_params=_cparams("parallel"),
        name="even_out",
    )(x, oc, osl, ow, om, wn, wm, g.reshape(1, -1), b.reshape(1, -1))


def _odd_out_body(x_ref, y_ref, w_ref, g_ref, b_ref, o_ref):
    y = _dot(y_ref[...].astype(BF16), w_ref[...])
    o_ref[...] = _layer_norm(ALPHA * x_ref[...] + y, g_ref[...], b_ref[...])


def odd_out(x, y, w, g, b):
    m = x.shape[0]
    tm = _pick_tile(m, 512)
    row = lambda i: (i, 0)
    return pl.pallas_call(
        _odd_out_body,
        grid=(m // tm,),
        in_specs=[pl.BlockSpec((tm, D_MODEL), row), pl.BlockSpec((tm, ODD_MIX), row),
                  pl.BlockSpec((ODD_MIX, D_MODEL), lambda i: (0, 0)),
                  pl.BlockSpec((1, D_MODEL), lambda i: (0, 0)),
                  pl.BlockSpec((1, D_MODEL), lambda i: (0, 0))],
        out_specs=pl.BlockSpec((tm, D_MODEL), row),
        out_shape=jax.ShapeDtypeStruct((m, D_MODEL), F32),
        compiler_params=_cparams("parallel"),
        name="odd_out",
    )(x, y, w, g.reshape(1, -1), b.reshape(1, -1))


def _cmp_combine_body(pt_ref, *refs, n_pages):
    f_refs = refs[:n_pages]
    pe_ref, w1_ref, w2k_ref, w2v_ref, o_ref = refs[n_pages:]
    f = jnp.concatenate([r[0] for r in f_refs], axis=0)
    n = f.shape[0]
    slab = jnp.zeros((n, 128), F32)
    for part, w2_ref in ((0, w2k_ref), (1, w2v_ref)):
        first = f[:, 256 * part:256 * part + 128]
        second = f[:, 256 * part + 128:256 * part + 256]
        pe_term = _dot(pe_ref[part], w1_ref[part])[0:1]
        hid = first + pltpu.roll(second, n - 1, 0) + pe_term
        slab = slab + _dot(jax.nn.gelu(hid).astype(BF16), w2_ref[...])
    o_ref[0] = slab.astype(BF16)


def cmp_combine(f_pages, table, pe, w1, w2k, w2v):
    n_seq, n_pages = table.shape
    idx = lambda i, s, pt: (pt[s * n_pages + i], 0, 0)
    full3 = lambda s, pt: (0, 0, 0)
    grid_spec = pltpu.PrefetchScalarGridSpec(
        num_scalar_prefetch=1,
        grid=(n_seq,),
        in_specs=[pl.BlockSpec((1, 8, 512), functools.partial(idx, i)) for i in range(n_pages)] + [
            pl.BlockSpec((2, 8, 2048), full3), pl.BlockSpec((2, 2048, 128), full3),
            pl.BlockSpec((128, 128), lambda s, pt: (0, 0)), pl.BlockSpec((128, 128), lambda s, pt: (0, 0))],
        out_specs=pl.BlockSpec((1, 8 * n_pages, 128), lambda s, pt: (s, 0, 0)),
    )
    return pl.pallas_call(
        functools.partial(_cmp_combine_body, n_pages=n_pages),
        grid_spec=grid_spec,
        out_shape=jax.ShapeDtypeStruct((n_seq, 8 * n_pages, 128), BF16),
        compiler_params=_cparams("arbitrary"),
        name="cmp_combine",
    )(table.reshape(-1), *([f_pages] * n_pages), pe, w1, w2k, w2v)


def _softmax_step(s_all, mask, distf, slopes, m_ref, l_ref, n_groups, tq, row0=0):
    nr = n_groups * tq
    tk = s_all.shape[1]
    rows = slice(row0, row0 + nr)
    m_prev = m_ref[rows]
    l_prev = l_ref[rows]
    m_out, l_out, a_out, p_out = [], [], [], []
    for r in range(n_groups):
        sub = slice(r * tq, (r + 1) * tq)
        s = s_all[sub]
        if slopes is not None:
            s = s - slopes[r] * distf
        s = jnp.where(mask, s, MASKED)
        m_new = jnp.maximum(m_prev[sub], jnp.broadcast_to(jnp.max(s, axis=1, keepdims=True), (tq, LANES)))
        a = jnp.exp(m_prev[sub] - m_new)
        ps = [jnp.exp(s[:, c:c + LANES] - m_new) for c in range(0, tk, LANES)]
        l_new = a * l_prev[sub] + ps[0]
        for pc in ps[1:]:
            l_new = l_new + pc
        m_out.append(m_new)
        l_out.append(l_new)
        a_out.append(a)
        p_out.append(jnp.concatenate([pc.astype(BF16) for pc in ps], axis=1))
    cat = lambda xs: xs[0] if len(xs) == 1 else jnp.concatenate(xs, axis=0)
    m_ref[rows] = cat(m_out)
    l_ref[rows] = cat(l_out)
    return cat(a_out), cat(p_out)


def _rescale(a, acc):
    w = acc.shape[1]
    if w <= LANES:
        return acc * a[:, 0:w]
    return jnp.concatenate([acc[:, c:c + LANES] * a for c in range(0, w, LANES)], axis=1)


def _inv_or_zero(l):
    return jnp.where(l > 0.0, 1.0 / l, 0.0)


def _finish_mla(acc, l, wuv_ref, o_ref, tq):
    ol = (acc * _inv_or_zero(l)).astype(BF16)
    out = _dot(ol[0:tq], wuv_ref[0])
    for h in range(1, MLA_HEADS):
        out = out + _dot(ol[h * tq:(h + 1) * tq], wuv_ref[h])
    o_ref[...] = out


def _finish_nsa(acc, l, gate, branch, o_ref, tq):
    o = acc * _inv_or_zero(l)
    if o.shape[1] == NSA_DK:
        o = jnp.concatenate([jnp.zeros_like(o), o], axis=1)
    for h in range(NSA_HEADS):
        c = GATE_LANE0 + NSA_HEADS * branch + h
        o_ref[h] = o[h * tq:(h + 1) * tq] * gate[:, c:c + 1]


def _finish_diff(acc, l, lam_ref, sub_ref, lam_init, o_ref, tq, col0=0):
    lp = lam_ref[...]
    lam = (jnp.exp(jnp.sum(lp[0:1] * lp[1:2], axis=1, keepdims=True))
           - jnp.exp(jnp.sum(lp[2:3] * lp[3:4], axis=1, keepdims=True)) + lam_init)
    o = acc * _inv_or_zero(l)
    half = DIFF_REP * tq
    d = o[0:half] - lam * o[half:2 * half]
    d = _rms_norm(d, sub_ref[...]) * (1.0 - lam_init)
    for r in range(DIFF_REP):
        o_ref[:, col0 + 128 * r:col0 + 128 * r + 128] = d[r * tq:(r + 1) * tq]


def _alibi(n_heads, h):
    return 2.0 ** (-8.0 * (h + 1) / n_heads)


def _prompt_flash_body(qi_ref, kj_ref, fl_ref, ll_ref, *refs, mode, tq, tk, lam_init):
    refs = list(refs)
    q_ref = refs.pop(0)
    k_ref = refs.pop(0)
    v_ref = refs.pop(0) if mode == "diff" else None
    if mode == "slc":
        sel_ref = refs.pop(0)
        e_ref = refs.pop(0)
    if mode in ("slc", "win"):
        gate_ref = refs.pop(0)
    if mode == "mla":
        wuv_ref = refs.pop(0)
    if mode == "diff":
        lam_ref = refs.pop(0)
        sub_ref = refs.pop(0)
    o_ref, qb_ref, m_ref, l_ref, acc_ref = refs
    n_groups = q_ref.shape[1]
    g = pl.program_id(1)
    s = pl.program_id(2)

    @pl.when(fl_ref[s] == 1)
    def _():
        qb_ref[...] = q_ref[0].reshape(n_groups * tq, q_ref.shape[3]).astype(BF16)
        m_ref[...] = jnp.full_like(m_ref, M_INIT)
        l_ref[...] = jnp.zeros_like(l_ref)
        acc_ref[...] = jnp.zeros_like(acc_ref)

    kb = k_ref[...].astype(BF16)
    s_all = _dot_nt(qb_ref[...], kb)
    q_pos = qi_ref[s] * tq + lax.broadcasted_iota(jnp.int32, (tq, tk), 0)
    k_pos = kj_ref[s] * tk + lax.broadcasted_iota(jnp.int32, (tq, tk), 1)
    dist = q_pos - k_pos
    mask = dist >= 0
    if mode == "win":
        mask = jnp.logical_and(mask, dist <= WINDOW)
    if mode == "slc":
        picked = _dot(sel_ref[...].astype(BF16), e_ref[...])
        mask = jnp.logical_and(mask, picked > 0.5)
    if mode == "mla":
        slopes = None
    elif mode == "diff":
        s0 = [_alibi(DIFF_HEADS, r) for r in range(DIFF_REP)] * 2
        s1 = [_alibi(DIFF_HEADS, DIFF_REP + r) for r in range(DIFF_REP)] * 2
        slopes = [jnp.where(g == 0, a, b) for a, b in zip(s0, s1)]
    else:
        slopes = [_alibi(NSA_HEADS, h) for h in range(NSA_HEADS)]
    a, p = _softmax_step(s_all, mask, dist.astype(F32), slopes, m_ref, l_ref, n_groups, tq)
    vb = kb[:, 0:128] if mode != "diff" else v_ref[...].astype(BF16)
    acc_ref[...] = _rescale(a, acc_ref[...]) + _dot(p, vb)

    @pl.when(ll_ref[s] == 1)
    def _():
        acc, l = acc_ref[...], jnp.sum(l_ref[...], axis=1, keepdims=True)
        if mode == "mla":
            _finish_mla(acc, l, wuv_ref, o_ref, tq)
        elif mode == "diff":
            _finish_diff(acc, l, lam_ref, sub_ref, lam_init, o_ref, tq)
        else:
            _finish_nsa(acc, l, gate_ref[...], 1 if mode == "slc" else 2, o_ref, tq)


def _prompt_steps(seq, tq, tk, mode):
    qi, kj, fl, ll = [], [], [], []
    for i in range(seq // tq):
        hi = (i * tq + tq - 1) // tk
        lo = max(0, (i * tq - WINDOW) // tk) if mode == "win" else 0
        for j in range(lo, hi + 1):
            qi.append(i); kj.append(j); fl.append(int(j == lo)); ll.append(int(j == hi))
    return [jnp.asarray(np.asarray(a, np.int32)) for a in (qi, kj, fl, ll)]


def prompt_flash(mode, batch, seq, q, k, v=None, sel=None, emat=None, gate=None, wuv=None,
                 lam=None, sub=None, lam_init=0.0):
    n_g, n_groups, m, dk = q.shape
    tq, tk = (256, 256) if mode == "win" else (128, 2048)
    tq, tk = min(tq, seq), min(tk, seq)
    nq, nk = seq // tq, seq // tk
    steps = _prompt_steps(seq, tq, tk, mode)
    n_steps = int(steps[0].shape[0])
    qrow = lambda b, g, s, qi, kj, fl, ll: (b * nq + qi[s], 0)
    kcol = {"mla": 0, "slc": 1, "win": 0, "diff": None}[mode]
    kw = 256 if mode == "mla" else 128
    if mode == "diff":
        kmap = lambda b, g, s, qi, kj, fl, ll: (b * nk + kj[s], g)
    else:
        kmap = lambda b, g, s, qi, kj, fl, ll: (b * nk + kj[s], kcol)
    in_specs = [pl.BlockSpec((1, n_groups, tq, dk), lambda b, g, s, qi, kj, fl, ll: (g, 0, b * nq + qi[s], 0)),
                pl.BlockSpec((tk, kw), kmap)]
    args = [q, k]
    if mode == "diff":
        in_specs.append(pl.BlockSpec((tk, 128), lambda b, g, s, qi, kj, fl, ll: (b * nk + kj[s], 2 + g)))
        args.append(k)
    if mode == "slc":
        in_specs += [pl.BlockSpec((tq, 128), qrow),
                     pl.BlockSpec((128, tk), lambda b, g, s, qi, kj, fl, ll: (0, kj[s]))]
        args += [sel, emat]
    if mode in ("slc", "win"):
        in_specs.append(pl.BlockSpec((tq, 128), qrow))
        args.append(gate)
    if mode == "mla":
        in_specs.append(pl.BlockSpec((MLA_HEADS, 128, 512), lambda b, g, s, qi, kj, fl, ll: (0, 0, 0)))
        args.append(wuv)
    if mode == "diff":
        in_specs += [pl.BlockSpec((4, DIFF_D), lambda b, g, s, qi, kj, fl, ll: (0, 0)),
                     pl.BlockSpec((1, 128), lambda b, g, s, qi, kj, fl, ll: (0, 0))]
        args += [lam, sub]
    rows = batch * seq
    if mode == "mla":
        out_spec = pl.BlockSpec((tq, 512), qrow)
        out_shape = jax.ShapeDtypeStruct((rows, 512), F32)
    elif mode == "diff":
        out_spec = pl.BlockSpec((tq, 512), lambda b, g, s, qi, kj, fl, ll: (b * nq + qi[s], g))
        out_shape = jax.ShapeDtypeStruct((rows, ODD_MIX), F32)
    else:
        out_spec = pl.BlockSpec((NSA_HEADS, tq, 128), lambda b, g, s, qi, kj, fl, ll: (0, b * nq + qi[s], 0))
        out_shape = jax.ShapeDtypeStruct((NSA_HEADS, rows, 128), F32)
    nr = n_groups * tq
    grid_spec = pltpu.PrefetchScalarGridSpec(
        num_scalar_prefetch=4,
        grid=(batch, n_g, n_steps),
        in_specs=in_specs,
        out_specs=out_spec,
        scratch_shapes=[pltpu.VMEM((nr, dk), BF16), pltpu.VMEM((nr, LANES), F32), pltpu.VMEM((nr, LANES), F32),
                        pltpu.VMEM((nr, LANES), F32)],
    )
    return pl.pallas_call(
        functools.partial(_prompt_flash_body, mode=mode, tq=tq, tk=tk, lam_init=lam_init),
        grid_spec=grid_spec,
        out_shape=out_shape,
        compiler_params=_cparams("parallel", "parallel", "arbitrary"),
        name="prompt_flash_" + mode,
    )(*steps, *args)


def _cmp_select_body(q_ref, c_ref, cmap_ref, gate_ref, oc_ref, sel_ref, *, tq, n_sub, n_cmp, pos0_fn, k_eff):
    nc = c_ref.shape[1]
    nr = n_sub * tq
    pos0 = pos0_fn()
    col = lax.broadcasted_iota(jnp.int32, (tq, nc), 1)
    q_pos = pos0 + lax.broadcasted_iota(jnp.int32, (tq, nc), 0)
    dist = q_pos - (col * CMP_STRIDE + (CMP_LEN - 1))
    mask = jnp.logical_and(dist >= 0, col < n_cmp)
    distf = dist.astype(F32)
    cmap = cmap_ref[...]
    imps = []
    for u in range(n_sub):
        sub = slice(u * tq, (u + 1) * tq)
        qb = q_ref[0, :, sub, :].reshape(NSA_HEADS * tq, 128).astype(BF16)
        cb = c_ref[u]
        s_all = _dot_nt(qb, cb)
        gate = gate_ref[sub, :]
        psum = jnp.zeros((tq, nc), F32)
        for h in range(NSA_HEADS):
            s = s_all[h * tq:(h + 1) * tq] - _alibi(NSA_HEADS, h) * distf
            s = jnp.where(mask, s, MASKED)
            e = jnp.where(mask, jnp.exp(s - jnp.max(s, axis=1, keepdims=True)), 0.0)
            p = e * _inv_or_zero(jnp.sum(e, axis=1, keepdims=True))
            psum = psum + p
            c = GATE_LANE0 + h
            oc_ref[h, sub, :] = _dot(p.astype(BF16), cb) * gate[:, c:c + 1]
        hi = psum.astype(BF16)
        rem = psum - hi.astype(F32)
        mid = rem.astype(BF16)
        lo = (rem - mid.astype(F32)).astype(BF16)
        imps.append(_dot(hi, cmap) + _dot(mid, cmap) + _dot(lo, cmap))
    imp = imps[0] if n_sub == 1 else jnp.concatenate(imps, axis=0)
    blk = lax.broadcasted_iota(jnp.int32, (nr, LANES), 1)
    row = lax.broadcasted_iota(jnp.int32, (nr, LANES), 0)
    qp = pos0 + (row if n_sub == 1 else (row & (tq - 1)))
    cur = qp // SLC_LEN
    forced = (blk == 0) | (blk == cur) | (blk == cur - 1)
    valid = blk * SLC_LEN <= qp
    v = jnp.where(valid, jnp.where(forced, FORCED_SCORE, imp), MASKED)
    sel = jnp.zeros((nr, LANES), F32)
    blkf = blk.astype(F32)
    for _ in range(k_eff):
        top = jnp.max(v, axis=1, keepdims=True)
        first = jnp.min(jnp.where(v == top, blkf, float(LANES)), axis=1, keepdims=True)
        hit = blkf == first
        sel = jnp.where(hit, 1.0, sel)
        v = jnp.where(hit, -jnp.inf, v)
    sel_ref[...] = jnp.where(valid, sel, 0.0)


def cmp_select(qn, cslab, cmap, gate, *, rows0, n_rows, tq, n_sub, blocks_per_seq, n_cmp, pos_fixed, k_eff):
    nc = cslab.shape[1]
    br = n_sub * tq
    assert n_rows % br == 0 and rows0 % br == 0 and (n_sub == 1 or blocks_per_seq == 1)
    nblk = n_rows // br
    blk0 = rows0 // br
    if pos_fixed is None:
        pos0_fn = lambda: (pl.program_id(0) % blocks_per_seq) * tq
    else:
        pos0_fn = lambda: pos_fixed
    return pl.pallas_call(
        functools.partial(_cmp_select_body, tq=tq, n_sub=n_sub, n_cmp=n_cmp, pos0_fn=pos0_fn, k_eff=k_eff),
        grid=(nblk,),
        in_specs=[pl.BlockSpec((1, NSA_HEADS, br, 128), lambda i: (0, 0, blk0 + i, 0)),
                  pl.BlockSpec((n_sub, nc, 128), lambda i: (i // blocks_per_seq, 0, 0)),
                  pl.BlockSpec((nc, 128), lambda i: (0, 0)),
                  pl.BlockSpec((br, 128), lambda i: (blk0 + i, 0))],
        out_specs=[pl.BlockSpec((NSA_HEADS, br, 128), lambda i: (0, i, 0)),
                   pl.BlockSpec((br, 128), lambda i: (i, 0))],
        out_shape=[jax.ShapeDtypeStruct((NSA_HEADS, n_rows, 128), F32),
                   jax.ShapeDtypeStruct((n_rows, 128), F32)],
        compiler_params=_cparams("parallel"),
        name="cmp_select",
    )(qn, cslab, cmap, gate)


def _cmp_paged_body(pt_ref, *refs, pp):
    page_refs = refs[:pp]
    perm_ref, w_ref, o_ref, s_ref = refs[pp:]
    z = _dot(jnp.concatenate([p[0].reshape(2 * NSA_DK, PAGE_SIZE) for p in page_refs], axis=0).astype(BF16),
             perm_ref[...])
    for i in range(pp):
        zt = z[PAGE_SIZE * i:PAGE_SIZE * (i + 1), :].T
        for r in range(CMP_STRIDE):
            s_ref[r, 8 * i:8 * (i + 1), :] = zt[8 * r:8 * (r + 1), :]
    acc = jnp.zeros((8 * pp, 4 * CMP_HIDDEN), F32)
    for q in range(CMP_STRIDE // 2):
        x = jnp.concatenate([s_ref[2 * q], s_ref[2 * q + 1]], axis=1).astype(BF16)
        acc = acc + _dot(x, w_ref[q])
    o_ref[...] = acc


def cmp_paged(pages, table, w, pp):
    n_seq, n_pages = table.shape
    n_st = n_pages // pp

    def page_map(i, sq, st, pt):
        return (pt[sq * n_pages + st * pp + i], 0, 0, 0)

    grid_spec = pltpu.PrefetchScalarGridSpec(
        num_scalar_prefetch=1,
        grid=(n_seq, n_st),
        in_specs=[pl.BlockSpec((1, 2, NSA_DK, PAGE_SIZE), functools.partial(page_map, i)) for i in range(pp)] + [
            pl.BlockSpec((PAGE_SIZE, PAGE_SIZE), lambda sq, st, pt: (0, 0)),
            pl.BlockSpec((CMP_STRIDE // 2, 4 * NSA_DK, 4 * CMP_HIDDEN), lambda sq, st, pt: (0, 0, 0))],
        out_specs=pl.BlockSpec((8 * pp, 512), lambda sq, st, pt: (sq * n_st + st, 0)),
        scratch_shapes=[pltpu.VMEM((CMP_STRIDE, 8 * pp, 2 * NSA_DK), F32)],
    )
    tok = np.arange(PAGE_SIZE)
    perm = np.zeros((PAGE_SIZE, PAGE_SIZE), np.float32)
    perm[tok, (tok % CMP_STRIDE) * (PAGE_SIZE // CMP_STRIDE) + tok // CMP_STRIDE] = 1.0
    return pl.pallas_call(
        functools.partial(_cmp_paged_body, pp=pp),
        grid_spec=grid_spec,
        out_shape=jax.ShapeDtypeStruct((n_seq * n_pages * 8, 512), F32),
        compiler_params=_cparams("parallel", "arbitrary"),
        name="cmp_paged",
    )(table.reshape(-1), *([pages] * pp), jnp.asarray(perm).astype(BF16), w)


def _stack_pages(mode, page_refs, g, dq):
    if mode == "mla":
        kt = jnp.concatenate([r[0] for r in page_refs], axis=1)
        kt = jnp.concatenate([kt, jnp.zeros((dq - kt.shape[0], kt.shape[1]), F32)], axis=0).astype(BF16)
        return kt, kt[0:MLA_KV_RANK], True
    if mode in ("slc", "win"):
        return (jnp.concatenate([r[0, 0] for r in page_refs], axis=1).astype(BF16),
                jnp.concatenate([r[0, 1] for r in page_refs], axis=1).astype(BF16), True)
    return (jnp.concatenate([r[0, pl.ds(g, PAGE_SIZE, stride=4), :] for r in page_refs], axis=0).astype(BF16),
            jnp.concatenate([r[0, pl.ds(2 + g, PAGE_SIZE, stride=4), :] for r in page_refs], axis=0).astype(BF16),
            False)


def _sample_flash_body(pt_ref, *refs, mode, pp, ptok, t_new, past, kpos0, lam_init):
    refs = list(refs)
    q_ref = refs.pop(0)
    page_refs = [refs.pop(0) for _ in range(pp)]
    kn_ref = refs.pop(0)
    if mode == "slc":
        sel_ref = refs.pop(0)
        e_ref = refs.pop(0)
    if mode in ("slc", "win"):
        gate_ref = refs.pop(0)
    if mode == "mla":
        wuv_ref = refs.pop(0)
    if mode == "diff":
        lam_ref = refs.pop(0)
        sub_ref = refs.pop(0)
    o_ref, qb_ref, m_ref, l_ref, acc_ref, kpad_ref, vpad_ref = refs
    n_g, n_groups = q_ref.shape[0], q_ref.shape[1]
    tq = t_new
    nr = n_groups * tq
    st = pl.program_id(1)
    n_st = pl.num_programs(1)
    tk = pp * ptok
    dq = qb_ref.shape[1]

    @pl.when(st == 0)
    def _():
        q = q_ref[...].reshape(n_g * nr, q_ref.shape[3])
        qb_ref[...] = q[:, 0:dq].astype(BF16)
        m_ref[...] = jnp.full_like(m_ref, M_INIT)
        l_ref[...] = jnp.zeros_like(l_ref)
        acc_ref[...] = jnp.zeros_like(acc_ref)

    assert tq & (tq - 1) == 0
    head_col = lax.broadcasted_iota(jnp.int32, (nr, 1), 0) >> (tq.bit_length() - 1)

    def slopes_of(g):
        if mode == "mla":
            return None
        col = jnp.zeros((nr, 1), F32)
        for r in range(n_groups):
            if mode == "diff":
                sl = _alibi(DIFF_HEADS, g * DIFF_REP + r % DIFF_REP)
            else:
                sl = _alibi(NSA_HEADS, r)
            col = jnp.where(head_col == r, sl, col)
        return [col]

    def dist_and_mask(width, k_pos0, k_limit):
        row = lax.broadcasted_iota(jnp.int32, (nr, width), 0)
        colk = lax.broadcasted_iota(jnp.int32, (nr, width), 1)
        dist = (past + (row & (tq - 1))) - (k_pos0 + colk)
        mask = dist >= 0
        if k_limit is not None:
            mask = jnp.logical_and(mask, colk < k_limit)
        if mode == "win":
            mask = jnp.logical_and(mask, dist <= WINDOW)
        return dist, mask

    dist, mask = dist_and_mask(tk, kpos0 + st * tk, None)
    if mode == "slc":
        picked = _dot(sel_ref[...].astype(BF16), e_ref[...])
        mask = jnp.logical_and(mask, jnp.concatenate([picked] * n_groups, axis=0) > 0.5)
    distf = dist.astype(F32)
    for g in range(n_g):
        rows = slice(g * nr, (g + 1) * nr)
        qb = qb_ref[rows]
        kb, vb, tokens_last = _stack_pages(mode, page_refs, g, dq)
        s_all = _dot(qb, kb) if tokens_last else _dot_nt(qb, kb)
        a, p = _softmax_step(s_all, mask, distf, slopes_of(g), m_ref, l_ref, 1, nr, row0=g * nr)
        pv = _dot_nt(p, vb) if tokens_last else _dot(p, vb)
        acc_ref[rows] = _rescale(a, acc_ref[rows]) + pv

    @pl.when(st == n_st - 1)
    def _():
        dn, mn = dist_and_mask(LANES, past, t_new)
        dnf = dn.astype(F32)
        kn = kn_ref[...]
        for g in range(n_g):
            rows = slice(g * nr, (g + 1) * nr)
            if mode == "diff":
                k_new, v_new = kn[:, 128 * g:128 * g + 128], kn[:, 256 + 128 * g:384 + 128 * g]
            elif mode == "mla":
                k_new, v_new = kn, kn[:, 0:128]
            else:
                k_new, v_new = kn[:, 0:NSA_DK], kn[:, NSA_DK:2 * NSA_DK]
            kpad_ref[...] = jnp.zeros_like(kpad_ref)
            kpad_ref[0:t_new, :] = k_new.astype(BF16)
            vpad_ref[...] = jnp.zeros_like(vpad_ref)
            vpad_ref[0:t_new, :] = v_new.astype(BF16)
            s_new = _dot_nt(qb_ref[rows], kpad_ref[...])
            a, p = _softmax_step(s_new, mn, dnf, slopes_of(g), m_ref, l_ref, 1, nr, row0=g * nr)
            acc = _rescale(a, acc_ref[rows]) + _dot(p, vpad_ref[...])
            l = jnp.sum(l_ref[rows], axis=1, keepdims=True)
            if mode == "mla":
                _finish_mla(acc, l, wuv_ref, o_ref, tq)
            elif mode == "diff":
                _finish_diff(acc, l, lam_ref, sub_ref, lam_init, o_ref, tq, col0=512 * g)
            else:
                _finish_nsa(acc, l, gate_ref[...], 1 if mode == "slc" else 2, o_ref, tq)


def sample_flash(mode, n_seq, t_new, row0, past, q, pages, table, knew, kpos0, pp, sel=None, emat=None,
                 gate=None, wuv=None, lam=None, sub=None, lam_init=0.0):
    n_g, n_groups, m, dk = q.shape
    n_pages = table.shape[1]
    assert n_pages % pp == 0 and row0 % t_new == 0
    n_st = n_pages // pp
    blk0 = row0 // t_new
    kn_w, kn_col = {"mla": (256, 0), "slc": (128, 1), "win": (128, 0), "diff": (512, 0)}[mode]
    dq, dkk, dv = {"mla": (256, 256, 128), "slc": (64, 64, 64), "win": (64, 64, 64), "diff": (128, 128, 128)}[mode]
    if mode in ("slc", "win"):
        page_block = (1, 2) + pages.shape[2:]
        ptok = pages.shape[3]
    else:
        page_block = (1,) + pages.shape[1:]
        ptok = PAGE_SIZE
    part_blk = 1 if mode == "slc" else 0

    def page_map(i, sq, st, pt):
        return (pt[sq * n_pages + st * pp + i], part_blk) + (0,) * (len(page_block) - 2)

    in_specs = [pl.BlockSpec((n_g, n_groups, t_new, dk), lambda sq, st, pt: (0, 0, blk0 + sq, 0))]
    args = [q]
    in_specs += [pl.BlockSpec(page_block, functools.partial(page_map, i)) for i in range(pp)]
    args += [pages] * pp
    in_specs.append(pl.BlockSpec((t_new, kn_w), lambda sq, st, pt: (blk0 + sq, kn_col)))
    args.append(knew)
    if mode == "slc":
        in_specs += [pl.BlockSpec((t_new, 128), lambda sq, st, pt: (sq, 0)),
                     pl.BlockSpec((128, pp * PAGE_SIZE), lambda sq, st, pt: (0, st))]
        args += [sel, emat]
    if mode in ("slc", "win"):
        in_specs.append(pl.BlockSpec((t_new, 128), lambda sq, st, pt: (blk0 + sq, 0)))
        args.append(gate)
    if mode == "mla":
        in_specs.append(pl.BlockSpec((MLA_HEADS, 128, 512), lambda sq, st, pt: (0, 0, 0)))
        args.append(wuv)
    if mode == "diff":
        in_specs += [pl.BlockSpec((4, DIFF_D), lambda sq, st, pt: (0, 0)),
                     pl.BlockSpec((1, 128), lambda sq, st, pt: (0, 0))]
        args += [lam, sub]
    rows = n_seq * t_new
    if mode == "mla":
        out_spec = pl.BlockSpec((t_new, 512), lambda sq, st, pt: (sq, 0))
        out_shape = jax.ShapeDtypeStruct((rows, 512), F32)
    elif mode == "diff":
        out_spec = pl.BlockSpec((t_new, ODD_MIX), lambda sq, st, pt: (sq, 0))
        out_shape = jax.ShapeDtypeStruct((rows, ODD_MIX), F32)
    else:
        out_spec = pl.BlockSpec((NSA_HEADS, t_new, 128), lambda sq, st, pt: (0, sq, 0))
        out_shape = jax.ShapeDtypeStruct((NSA_HEADS, rows, 128), F32)
    nr = n_g * n_groups * t_new
    grid_spec = pltpu.PrefetchScalarGridSpec(
        num_scalar_prefetch=1,
        grid=(n_seq, n_st),
        in_specs=in_specs,
        out_specs=out_spec,
        scratch_shapes=[pltpu.VMEM((nr, dq), BF16), pltpu.VMEM((nr, LANES), F32), pltpu.VMEM((nr, LANES), F32),
                        pltpu.VMEM((nr, dv), F32),
                        pltpu.VMEM((LANES, dkk), BF16), pltpu.VMEM((LANES, dv), BF16)],
    )
    return pl.pallas_call(
        functools.partial(_sample_flash_body, mode=mode, pp=pp, ptok=ptok, t_new=t_new, past=past, kpos0=kpos0,
                          lam_init=lam_init),
        grid_spec=grid_spec,
        out_shape=out_shape,
        compiler_params=_cparams("parallel", "arbitrary"),
        name="sample_flash_" + mode,
    )(table.reshape(-1), *args)


def _even_weights(w_in, gate_b, w_uq, w_uk, w_uv, w_out, cmp_w1, cmp_w2, cmp_pe):
    cuts = np.cumsum(EVEN_SPLITS)[:-1].tolist()
    q_nsa, kv_nsa, kv_win, gate, q_lat, kv_lat, k_rope = jnp.split(w_in, cuts, axis=1)
    pad = jnp.zeros((D_MODEL, EVEN_IN_PAD - sum(EVEN_SPLITS)), F32)
    w = jnp.concatenate([q_nsa, kv_nsa, kv_win, q_lat, kv_lat, k_rope, gate, pad], axis=1).astype(BF16)
    gb = jnp.zeros((1, 128), F32).at[0, GATE_LANE0:GATE_LANE0 + 3 * NSA_HEADS].set(gate_b)
    uq = w_uq.reshape(MLA_Q_RANK, MLA_HEADS, MLA_NOPE + MLA_ROPE)
    wuq = jnp.concatenate([uq[:, :, :MLA_NOPE].reshape(MLA_Q_RANK, -1),
                           uq[:, :, MLA_NOPE:].reshape(MLA_Q_RANK, -1)], axis=1).astype(BF16)
    wuk = jnp.zeros((MLA_HEADS, MLA_NOPE, MLA_HEADS, MLA_KV_RANK), F32)
    wuv = jnp.zeros((MLA_HEADS, MLA_KV_RANK, MLA_HEADS, MLA_V), F32)
    for h in range(MLA_HEADS):
        wuk = wuk.at[h, :, h, :].set(w_uk[h].T)
        wuv = wuv.at[h, :, h, :].set(w_uv[h])
    wuk = wuk.reshape(MLA_HEADS * MLA_NOPE, MLA_HEADS * MLA_KV_RANK).astype(BF16)
    wuv = wuv.reshape(MLA_HEADS, MLA_KV_RANK, MLA_HEADS * MLA_V).astype(BF16)
    n_nsa = NSA_HEADS * NSA_DK
    wn = jnp.concatenate([jnp.zeros((NSA_HEADS, NSA_DK, D_MODEL), F32),
                          w_out[:n_nsa].reshape(NSA_HEADS, NSA_DK, D_MODEL)], axis=1).astype(BF16)
    wm = w_out[n_nsa:].astype(BF16)
    w1r = cmp_w1.reshape(2, 2, CMP_STRIDE, NSA_DK, CMP_HIDDEN)
    wc = jnp.zeros((CMP_STRIDE, 2, NSA_DK, 2, 2, CMP_HIDDEN), F32)
    for part in range(2):
        for half in range(2):
            wc = wc.at[:, part, :, part, half, :].set(w1r[part, half])
    wc = wc.reshape(CMP_STRIDE * 2 * NSA_DK, 4 * CMP_HIDDEN).astype(BF16)
    pe = jnp.zeros((2, 8, CMP_LEN * NSA_DK), F32).at[:, 0, :].set(cmp_pe.reshape(2, -1)).astype(BF16)
    w1 = cmp_w1.reshape(2, CMP_LEN * NSA_DK, CMP_HIDDEN).astype(BF16)
    zeros = jnp.zeros((CMP_HIDDEN, NSA_DK), F32)
    w2k = jnp.concatenate([cmp_w2[0], zeros], axis=1).astype(BF16)
    w2v = jnp.concatenate([zeros, cmp_w2[1]], axis=1).astype(BF16)
    return dict(w=w, gb=gb, wuq=wuq, wuk=wuk, wuv=wuv, wn=wn, wm=wm, wc=wc, pe=pe, w1=w1, w2k=w2k, w2v=w2v)


def _rope_tables(pos):
    half = MLA_ROPE // 2
    inv = ROPE_THETA ** (-jnp.arange(half, dtype=F32) / half)
    ang = pos.astype(F32)[:, None] * inv[None, :]
    cos, sin = jnp.cos(ang), jnp.sin(ang)
    cos = jnp.tile(jnp.concatenate([cos, cos], axis=1), (1, 4))
    sin = jnp.tile(jnp.concatenate([-sin, sin], axis=1), (1, 4))
    return cos, sin


def _cmap(n_rows, n_cmp):
    cs = np.arange(n_rows)[:, None] * CMP_STRIDE
    bs = np.arange(LANES)[None, :] * SLC_LEN
    m = (cs < bs + SLC_LEN) & (cs + CMP_LEN > bs) & (np.arange(n_rows)[:, None] < n_cmp)
    return jnp.asarray(m.astype(np.float32)).astype(BF16)


def _emat(n_keys):
    m = (np.arange(n_keys)[None, :] // SLC_LEN) == np.arange(LANES)[:, None]
    return jnp.asarray(m.astype(np.float32)).astype(BF16)


def kernel(x_prompt, x_sample, cache_nsa, cache_nsa_win, cache_mla, cache_diff, page_table, ln_g, ln_b, ffn_w_in, ffn_w_out, even_w_in, nsa_gate_b, nsa_cmp_pe, nsa_cmp_w1, nsa_cmp_w2, mla_q_norm_g, mla_w_uq, mla_kv_norm_g, mla_w_uk, mla_w_uv, even_w_out, odd_w_in, diff_lambda, diff_subln_g, odd_w_out):
    batch, seq, _ = x_prompt.shape
    n_seq, t_new, _ = x_sample.shape
    n_pages = page_table.shape[1]
    past = n_pages * PAGE_SIZE
    n_pool = cache_nsa.shape[1]
    rows_p = batch * seq
    rows_s = n_seq * t_new
    n_win = cache_nsa_win.shape[2]
    assert seq % 512 == 0 and past % 512 == 0 and n_win % PAGE_SIZE == 0 and t_new == 8

    x = jnp.concatenate([x_prompt.reshape(rows_p, D_MODEL), x_sample.reshape(rows_s, D_MODEL)], axis=0)
    pos = jnp.concatenate([jnp.tile(jnp.arange(seq, dtype=jnp.int32), batch),
                           jnp.tile(past + jnp.arange(t_new, dtype=jnp.int32), n_seq)])
    cos, sin = _rope_tables(pos)
    w_in_bf = ffn_w_in.astype(BF16)
    w_out_bf = ffn_w_out.astype(BF16)
    table = page_table.astype(jnp.int32)
    ident_p = jnp.arange(batch * (seq // PAGE_SIZE), dtype=jnp.int32).reshape(batch, seq // PAGE_SIZE)
    emat_p = _emat(seq)
    emat_s = _emat(past)
    cmap_p = _cmap(seq // CMP_STRIDE, seq // CMP_STRIDE - 1)
    cmap_s = _cmap(past // CMP_STRIDE, past // CMP_STRIDE - 1)
    pages_per_step = lambda cap: max(d for d in range(1, min(cap, n_pages) + 1) if n_pages % d == 0)
    pp = pages_per_step(64)
    pp_diff = pages_per_step(32)
    pp_cmp = pages_per_step(64)
    ident_s = jnp.arange(n_seq * n_pages, dtype=jnp.int32).reshape(n_seq, n_pages)
    nsa_pages = jnp.transpose(cache_nsa, (0, 1, 3, 4, 2)).reshape(-1, 4, NSA_DK, PAGE_SIZE)
    win_pages = jnp.transpose(cache_nsa_win, (0, 1, 3, 4, 2)).reshape(-1, 2, NSA_DK, n_win)
    mla_pages = jnp.transpose(cache_mla, (0, 1, 3, 2)).reshape(-1, MLA_KV_RANK + MLA_ROPE, PAGE_SIZE)
    diff_pages = cache_diff.reshape(-1, 4 * PAGE_SIZE, 2 * DIFF_D)
    ident_w = jnp.arange(n_seq, dtype=jnp.int32).reshape(n_seq, 1)

    outs = {k: [] for k in ("nsa", "win", "mla", "diff")}
    for li in range(DEPTH):
        j = li // 2
        x = ffn_ln(x, w_in_bf[li, 0], w_out_bf[li, 0], ln_g[li, 0], ln_b[li, 0])
        if li % 2 == 0:
            ew = _even_weights(even_w_in[j], nsa_gate_b[j], mla_w_uq[j], mla_w_uk[j], mla_w_uv[j], even_w_out[j],
                               nsa_cmp_w1[j], nsa_cmp_w2[j], nsa_cmp_pe[j])
            kvn, kvw, gate, qn, rows, mk, qm = even_proj(
                x, ew["w"], cos, sin, ew["gb"], mla_q_norm_g[j].reshape(1, -1), mla_kv_norm_g[j].reshape(1, -1),
                ew["wuq"], ew["wuk"])
            qn4 = qn[None]
            qm4 = qm[None]
            chunks_p = kvn[:rows_p, :128].reshape(rows_p // CMP_STRIDE, CMP_STRIDE * 128).astype(BF16)
            f_p = matmul(chunks_p, ew["wc"]).reshape(-1, 8, 512)
            c_p = cmp_combine(f_p, ident_p, ew["pe"], ew["w1"], ew["w2k"], ew["w2v"])
            table_j = table + j * n_pool
            f_s = cmp_paged(nsa_pages, table_j, ew["wc"].reshape(CMP_STRIDE // 2, 4 * NSA_DK, 4 * CMP_HIDDEN),
                            pp_cmp).reshape(n_seq * n_pages, 8, 512)
            c_s = cmp_combine(f_s, ident_s, ew["pe"], ew["w1"], ew["w2k"], ew["w2v"])

            oc_p, sel_p = cmp_select(qn4, c_p, cmap_p, gate, rows0=0, n_rows=rows_p, tq=128, n_sub=1,
                                     blocks_per_seq=seq // 128, n_cmp=seq // CMP_STRIDE - 1, pos_fixed=None,
                                     k_eff=N_SELECT)
            oc_s, sel_s = cmp_select(qn4, c_s, cmap_s, gate, rows0=rows_p, n_rows=rows_s, tq=t_new,
                                     n_sub=8 if n_seq % 8 == 0 and rows_p % (8 * t_new) == 0 else 1,
                                     blocks_per_seq=1, n_cmp=past // CMP_STRIDE - 1, pos_fixed=past,
                                     k_eff=N_SELECT - 1)
            os_p = prompt_flash("slc", batch, seq, qn4, kvn, sel=sel_p, emat=emat_p, gate=gate)
            ow_p = prompt_flash("win", batch, seq, qn4, kvw, gate=gate)
            om_p = prompt_flash("mla", batch, seq, qm4, mk, wuv=ew["wuv"])
            os_s = sample_flash("slc", n_seq, t_new, rows_p, past, qn4, nsa_pages, table_j, kvn, 0, pp,
                                sel=sel_s, emat=emat_s, gate=gate)
            ow_s = sample_flash("win", n_seq, t_new, rows_p, past, qn4, win_pages, ident_w + j * n_seq, kvw,
                                past - n_win, 1, gate=gate)
            om_s = sample_flash("mla", n_seq, t_new, rows_p, past, qm4, mla_pages, table_j, mk, 0, pp,
                                wuv=ew["wuv"])
            cat1 = lambda a, b: jnp.concatenate([a, b], axis=1)
            x = even_out(x, cat1(oc_p, oc_s), cat1(os_p, os_s), cat1(ow_p, ow_s),
                         jnp.concatenate([om_p, om_s], axis=0), ew["wn"], ew["wm"], ln_g[li, 1], ln_b[li, 1])
            outs["nsa"].append(kvn)
            outs["win"].append(kvw)
            outs["mla"].append(rows)
        else:
            lam_init = 0.8 - 0.6 * math.exp(-0.3 * li)
            kv, qd = odd_proj(x, odd_w_in[j].astype(BF16))
            sub = diff_subln_g[j].reshape(1, -1)
            od_p = prompt_flash("diff", batch, seq, qd, kv, lam=diff_lambda[j], sub=sub, lam_init=lam_init)
            od_s = sample_flash("diff", n_seq, t_new, rows_p, past, qd, diff_pages, table + j * n_pool, kv, 0, pp_diff,
                                lam=diff_lambda[j], sub=sub, lam_init=lam_init)
            x = odd_out(x, jnp.concatenate([od_p, od_s], axis=0), odd_w_out[j].astype(BF16),
                        ln_g[li, 1], ln_b[li, 1])
            outs["diff"].append(kv)
        x = ffn_ln(x, w_in_bf[li, 1], w_out_bf[li, 1], ln_g[li, 2], ln_b[li, 2])

    def split(a, shape_p, shape_s):
        return a[:rows_p].reshape(shape_p), a[rows_p:].reshape(shape_s)

    y_p, y_s = split(x, (batch, seq, D_MODEL), (n_seq, t_new, D_MODEL))
    nsa = [split(a, (batch, seq, 4, NSA_DK), (n_seq, t_new, 4, NSA_DK)) for a in outs["nsa"]]
    win = [split(a, (batch, seq, 2, NSA_DK), (n_seq, t_new, 2, NSA_DK)) for a in outs["win"]]
    mla = [split(a, (batch, seq, MLA_KV_RANK + MLA_ROPE), (n_seq, t_new, MLA_KV_RANK + MLA_ROPE))
           for a in outs["mla"]]
    dif = [split(a, (batch, seq, 2, DIFF_KV_HEADS, 2 * DIFF_D), (n_seq, t_new, 2, DIFF_KV_HEADS, 2 * DIFF_D))
           for a in outs["diff"]]
    n_keep = min(WINDOW, seq)
    win_p = jnp.stack([w[0][:, seq - n_keep:] for w in win], 0)
    win_s = jnp.stack([jnp.concatenate([cache_nsa_win[i][:, t_new:], w[1]], axis=1) for i, w in enumerate(win)], 0)
    return (y_p, y_s,
            jnp.stack([a[0] for a in nsa], 0), jnp.stack([a[1] for a in nsa], 0),
            win_p, win_s,
            jnp.stack([a[0] for a in mla], 0), jnp.stack([a[1] for a in mla], 0),
            jnp.stack([a[0] for a in dif], 0), jnp.stack([a[1] for a in dif], 0))
```
